```python
import math
import jax, jax.numpy as jnp
from jax import lax
import numpy as np

D_MODEL = 4096
BATCH = 2
SEQ = 8192
DEPTH = 2

MEM_LEN = 256
MIX_WIDTH = D_MODEL
DIFF_WIDTH = MIX_WIDTH // 2
FOURIER_WIDTH = MIX_WIDTH // 4
MEM_WIDTH = MIX_WIDTH - DIFF_WIDTH - FOURIER_WIDTH
DIFF_HEAD_DIM = 128
DIFF_HEADS = DIFF_WIDTH // (2 * DIFF_HEAD_DIM)
DIFF_V_DIM = 2 * DIFF_HEAD_DIM
FOURIER_GROUPS = 4
FOURIER_GROUP_DIM = FOURIER_WIDTH // FOURIER_GROUPS
MEM_HEADS = 4
MEM_HEAD_DIM = MEM_WIDTH // MEM_HEADS
Q_BLOCK = 128
RMS_EPS = 1e-6
LAMBDA_STD = 0.1
SPLIT_WIDTHS = (DIFF_WIDTH, DIFF_WIDTH, DIFF_WIDTH, DIFF_WIDTH,
                FOURIER_WIDTH, FOURIER_WIDTH,
                MEM_WIDTH, MEM_WIDTH)
IN_COLS = 4 * DIFF_WIDTH + 2 * FOURIER_WIDTH + 2 * MEM_WIDTH

kernel_name = 'hybrid_diffattn_fnet_memxattn_encoder'


def rms_norm(x, g):
    xf = x.astype(jnp.float32)
    y = xf * lax.rsqrt(jnp.mean(xf * xf, axis=-1, keepdims=True) + RMS_EPS)
    return (y * g.astype(jnp.float32)).astype(x.dtype)


def alibi_slopes(n_heads):
    return jnp.asarray([2.0 ** (-8.0 * (h + 1) / n_heads) for h in range(n_heads)], dtype=jnp.float32)


def diff_attention(q, k, v, lam, slopes):
    B, S = q.shape[0], q.shape[1]
    nb = S // Q_BLOCK
    scale = DIFF_HEAD_DIM ** -0.5
    q_blocks = jnp.moveaxis(q.reshape(B, nb, Q_BLOCK, DIFF_HEADS, 2, DIFF_HEAD_DIM), 1, 0)
    starts = jnp.arange(nb, dtype=jnp.int32) * Q_BLOCK
    k_pos = jnp.arange(S, dtype=jnp.int32)

    def one_block(args):
        qb, start = args
        s = jnp.einsum('bqhcd,bkhcd->bhcqk', qb, k, preferred_element_type=jnp.float32) * scale
        q_pos = start + jnp.arange(Q_BLOCK, dtype=jnp.int32)
        dist = jnp.abs(q_pos[:, None] - k_pos[None, :]).astype(jnp.float32)
        s = s - slopes[None, :, None, None, None] * dist[None, None, None]
        p = jax.nn.softmax(s, axis=-1)
        w = p[:, :, 0] - lam * p[:, :, 1]
        return jnp.einsum('bhqk,bkhe->bqhe', w.astype(v.dtype), v)

    out = lax.map(one_block, (q_blocks, starts))
    return jnp.moveaxis(out, 0, 1).reshape(B, S, DIFF_HEADS, DIFF_V_DIM)


def hybrid_layer(x, mem, g_in, w_in, g_mem, w_mem_k, w_mem_v, lam_q1, lam_k1, lam_q2, lam_k2,
                 g_subln, w_fourier, w_out, layer_idx, slopes):
    B, S, _ = x.shape
    h = rms_norm(x, g_in)
    proj = h @ w_in
    offsets = [int(o) for o in np.cumsum(SPLIT_WIDTHS)[:-1]]
    q_d, k_d, v_d, z_d, u_f, z_f, q_m, z_m = jnp.split(proj, offsets, axis=-1)

    lam_init = 0.8 - 0.6 * math.exp(-0.3 * layer_idx)
    lam = (jnp.exp(jnp.sum(lam_q1.astype(jnp.float32) * lam_k1.astype(jnp.float32)))
           - jnp.exp(jnp.sum(lam_q2.astype(jnp.float32) * lam_k2.astype(jnp.float32))) + lam_init)
    q = q_d.reshape(B, S, DIFF_HEADS, 2, DIFF_HEAD_DIM)
    k = k_d.reshape(B, S, DIFF_HEADS, 2, DIFF_HEAD_DIM)
    v = v_d.reshape(B, S, DIFF_HEADS, DIFF_V_DIM)
    o_d = diff_attention(q, k, v, lam, slopes)
    o_d = rms_norm(o_d, g_subln) * (1.0 - lam_init)
    y_d = o_d.reshape(B, S, DIFF_WIDTH) * jax.nn.silu(z_d)

    u = u_f.reshape(B, S, FOURIER_GROUPS, FOURIER_GROUP_DIM).astype(jnp.float32)
    f = jnp.fft.fft2(u, axes=(1, 3), norm='ortho').real.astype(x.dtype).reshape(B, S, FOURIER_WIDTH)
    y_f = (f @ w_fourier) * jax.nn.silu(z_f)

    m = rms_norm(mem, g_mem)
    km = (m @ w_mem_k).reshape(B, MEM_LEN, MEM_HEADS, MEM_HEAD_DIM)
    vm = (m @ w_mem_v).reshape(B, MEM_LEN, MEM_HEADS, MEM_HEAD_DIM)
    qm = q_m.reshape(B, S, MEM_HEADS, MEM_HEAD_DIM)
    sm = jnp.einsum('bshd,bmhd->bhsm', qm, km, preferred_element_type=jnp.float32) * (MEM_HEAD_DIM ** -0.5)
    pm = jax.nn.softmax(sm, axis=-1)
    o_m = jnp.einsum('bhsm,bmhd->bshd', pm.astype(vm.dtype), vm).reshape(B, S, MEM_WIDTH)
    y_m = o_m * jax.nn.silu(z_m)

    y = jnp.concatenate([y_d, y_f, y_m], axis=-1)
    return x + y @ w_out


def setup_inputs(seed: int = 0) -> dict:
    key = jax.random.key(seed)
    ks = jax.random.split(key, 16)
    f32 = jnp.float32
    x = jax.random.normal(ks[0], (BATCH, SEQ, D_MODEL), f32)
    mem = jax.random.normal(ks[1], (BATCH, MEM_LEN, D_MODEL), f32)
    g_in = 1.0 + 0.02 * jax.random.normal(ks[2], (DEPTH, D_MODEL), f32)
    w_in = jax.random.normal(ks[3], (DEPTH, D_MODEL, IN_COLS), f32) * D_MODEL ** -0.5
    g_mem = 1.0 + 0.02 * jax.random.normal(ks[4], (DEPTH, D_MODEL), f32)
    w_mem_k = jax.random.normal(ks[5], (DEPTH, D_MODEL, MEM_WIDTH), f32) * D_MODEL ** -0.5
    w_mem_v = jax.random.normal(ks[6], (DEPTH, D_MODEL, MEM_WIDTH), f32) * D_MODEL ** -0.5
    lam_q1 = LAMBDA_STD * jax.random.normal(ks[7], (DEPTH, DIFF_HEAD_DIM), f32)
    lam_k1 = LAMBDA_STD * jax.random.normal(ks[8], (DEPTH, DIFF_HEAD_DIM), f32)
    lam_q2 = LAMBDA_STD * jax.random.normal(ks[9], (DEPTH, DIFF_HEAD_DIM), f32)
    lam_k2 = LAMBDA_STD * jax.random.normal(ks[10], (DEPTH, DIFF_HEAD_DIM), f32)
    g_subln = 1.0 + 0.02 * jax.random.normal(ks[11], (DEPTH, DIFF_V_DIM), f32)
    w_fourier = jax.random.normal(ks[12], (DEPTH, FOURIER_WIDTH, FOURIER_WIDTH), f32) * FOURIER_WIDTH ** -0.5
    w_out = jax.random.normal(ks[13], (DEPTH, MIX_WIDTH, D_MODEL), f32) * MIX_WIDTH ** -0.5
    g_final = 1.0 + 0.02 * jax.random.normal(ks[14], (D_MODEL,), f32)
    return {'x': x, 'mem': mem, 'g_in': g_in, 'w_in': w_in, 'g_mem': g_mem, 'w_mem_k': w_mem_k,
            'w_mem_v': w_mem_v, 'lam_q1': lam_q1, 'lam_k1': lam_k1, 'lam_q2': lam_q2, 'lam_k2': lam_k2,
            'g_subln': g_subln, 'w_fourier': w_fourier, 'w_out': w_out, 'g_final': g_final}


def reference(x, mem, g_in, w_in, g_mem, w_mem_k, w_mem_v, lam_q1, lam_k1, lam_q2, lam_k2,
              g_subln, w_fourier, w_out, g_final):
    slopes = alibi_slopes(DIFF_HEADS)
    h = x
    for l in range(DEPTH):
        h = hybrid_layer(h, mem, g_in[l], w_in[l], g_mem[l], w_mem_k[l], w_mem_v[l],
                         lam_q1[l], lam_k1[l], lam_q2[l], lam_k2[l], g_subln[l],
                         w_fourier[l], w_out[l], l, slopes)
    return rms_norm(h, g_final)
```

```python
import functools
import math

import jax
import jax.numpy as jnp
from jax import lax
from jax.experimental import pallas as pl
from jax.experimental.pallas import tpu as pltpu

RMS_EPS = 1e-6
DIFF_HEAD_DIM = 128
DIFF_V_DIM = 2 * DIFF_HEAD_DIM
FOURIER_GROUPS = 4
MEM_HEADS = 4
DFT_SPLIT = 128

V7X_VMEM_BYTES = 64 * 1024 * 1024
VMEM_LIMIT_BYTES = V7X_VMEM_BYTES - 8 * 1024 * 1024

F32 = jnp.float32
BF16 = jnp.bfloat16


def _params(n_grid_dims):
    return pltpu.CompilerParams(dimension_semantics=("arbitrary",) * n_grid_dims,
                                vmem_limit_bytes=VMEM_LIMIT_BYTES)


def _silu(z):
    return z / (1.0 + jnp.exp(-z))


def _rmsnorm_body(x_ref, g_ref, o_ref):
    x = x_ref[...].astype(F32)
    y = x * lax.rsqrt(jnp.mean(x * x, axis=-1, keepdims=True) + RMS_EPS)
    o_ref[...] = (y * g_ref[...]).astype(o_ref.dtype)


def _rmsnorm(x, g, out_dtype, bm=256):
    m, d = x.shape
    bm = min(bm, m)
    return pl.pallas_call(
        _rmsnorm_body,
        grid=(m // bm,),
        in_specs=[pl.BlockSpec((bm, d), lambda i: (i, 0)),
                  pl.BlockSpec((1, d), lambda i: (0, 0))],
        out_specs=pl.BlockSpec((bm, d), lambda i: (i, 0)),
        out_shape=jax.ShapeDtypeStruct((m, d), out_dtype),
        compiler_params=_params(1),
        name="rmsnorm",
    )(x, g.reshape(1, d).astype(F32))


def _matmul_body(a_ref, w_ref, o_ref):
    o_ref[...] = jnp.dot(a_ref[...], w_ref[...], preferred_element_type=F32).astype(o_ref.dtype)


def _matmul(a, w, out_dtype, bm=1024, bn=1024, name="matmul"):
    m, k = a.shape
    n = w.shape[1]
    bm, bn = min(bm, m), min(bn, n)
    return pl.pallas_call(
        _matmul_body,
        grid=(m // bm, n // bn),
        in_specs=[pl.BlockSpec((bm, k), lambda i, j: (i, 0)),
                  pl.BlockSpec((k, bn), lambda i, j: (0, j))],
        out_specs=pl.BlockSpec((bm, bn), lambda i, j: (i, j)),
        out_shape=jax.ShapeDtypeStruct((m, n), out_dtype),
        compiler_params=_params(2),
        name=name,
    )(a, w)


def _matmul_acc_body(a_ref, b_ref, o_ref, acc_ref, *, nk):
    kk = pl.program_id(2)

    @pl.when(kk == 0)
    def _zero():
        acc_ref[...] = jnp.zeros_like(acc_ref)

    acc_ref[...] += jnp.dot(a_ref[...], b_ref[...], preferred_element_type=F32)

    @pl.when(kk == nk - 1)
    def _store():
        o_ref[...] = acc_ref[...].astype(o_ref.dtype)


def _matmul_acc(a, b, out_dtype, bm=1024, bn=1024, bk=2048, name="matmul_acc"):
    m, k = a.shape
    n = b.shape[1]
    bm, bn, bk = min(bm, m), min(bn, n), min(bk, k)
    nk = k // bk
    return pl.pallas_call(
        functools.partial(_matmul_acc_body, nk=nk),
        grid=(m // bm, n // bn, nk),
        in_specs=[pl.BlockSpec((bm, bk), lambda i, j, kk: (i, kk)),
                  pl.BlockSpec((bk, bn), lambda i, j, kk: (kk, j))],
        out_specs=pl.BlockSpec((bm, bn), lambda i, j, kk: (i, j)),
        out_shape=jax.ShapeDtypeStruct((m, n), out_dtype),
        scratch_shapes=[pltpu.VMEM((bm, bn), F32)],
        compiler_params=_params(3),
        name=name,
    )(a, b)


def _diff_attn_body(slopes_ref, q_ref, k_ref, v_ref, z_ref, lq1_ref, lk1_ref, lq2_ref, lk2_ref, g_ref,
                    o_ref, tbl_ref, m_ref, l_ref, acc_ref, *, blk, nk, lam_init):
    h = pl.program_id(1)
    qi = pl.program_id(2)
    ki = pl.program_id(3)
    slope = slopes_ref[h]

    @pl.when((qi == 0) & (ki == 0))
    def _fill_tables():
        row = lax.broadcasted_iota(jnp.int32, (blk, blk), 0)
        col = lax.broadcasted_iota(jnp.int32, (blk, blk), 1)
        d = (row - col).astype(F32) * slope
        tbl_ref[0] = d
        tbl_ref[1] = -d
        tbl_ref[2] = jnp.abs(d)

    @pl.when(ki == 0)
    def _reset_state():
        m_ref[...] = jnp.full_like(m_ref, -jnp.inf)
        l_ref[...] = jnp.zeros_like(l_ref)
        acc_ref[...] = jnp.zeros_like(acc_ref)

    sel = jnp.where(ki < qi, 0, jnp.where(ki > qi, 1, 2))
    c = slope * (jnp.abs(qi - ki) * blk).astype(F32)
    tbl = tbl_ref[sel]
    v = v_ref[...]
    for cm in range(2):
        lanes = slice(cm * DIFF_HEAD_DIM, (cm + 1) * DIFF_HEAD_DIM)
        s = lax.dot_general(q_ref[:, lanes], k_ref[:, lanes], (((1,), (1,)), ((), ())),
                            preferred_element_type=F32) - tbl
        m_old = m_ref[cm]
        m_new = jnp.maximum(m_old, jnp.max(s, axis=-1, keepdims=True) - c)
        p = jnp.exp(s - (m_new + c))
        alpha = jnp.exp(m_old - m_new)
        l_ref[cm] = alpha * l_ref[cm] + jnp.sum(p, axis=-1, keepdims=True)
        acc_ref[cm] = alpha * acc_ref[cm] + jnp.dot(p.astype(BF16), v, preferred_element_type=F32)
        m_ref[cm] = m_new

    @pl.when(ki == nk - 1)
    def _finish():
        lam = (jnp.exp(jnp.sum(lq1_ref[...] * lk1_ref[...], axis=-1, keepdims=True))
               - jnp.exp(jnp.sum(lq2_ref[...] * lk2_ref[...], axis=-1, keepdims=True)) + lam_init)
        o = acc_ref[0] * (1.0 / l_ref[0]) - lam * (acc_ref[1] * (1.0 / l_ref[1]))
        o = o * lax.rsqrt(jnp.mean(o * o, axis=-1, keepdims=True) + RMS_EPS)
        o = o * g_ref[...] * (1.0 - lam_init)
        o_ref[...] = (o * _silu(z_ref[...].astype(F32))).astype(o_ref.dtype)


def _diff_attention(proj, slopes, lam_q1, lam_k1, lam_q2, lam_k2, g_subln, *, batch, seq, heads, lam_init,
                    blk=1024):
    blk = min(blk, seq)
    nb = seq // blk
    hv = DIFF_V_DIM

    def rows_q(b, h, qi, ki, *_):
        return b * nb + qi

    def rows_k(b, h, qi, ki, *_):
        return b * nb + ki

    vec = lambda n: pl.BlockSpec((1, n), lambda b, h, qi, ki, *_: (0, 0))
    grid_spec = pltpu.PrefetchScalarGridSpec(
        num_scalar_prefetch=1,
        grid=(batch, heads, nb, nb),
        in_specs=[
            pl.BlockSpec((blk, hv), lambda b, h, qi, ki, *_: (rows_q(b, h, qi, ki), h)),
            pl.BlockSpec((blk, hv), lambda b, h, qi, ki, *_: (rows_k(b, h, qi, ki), heads + h)),
            pl.BlockSpec((blk, hv), lambda b, h, qi, ki, *_: (rows_k(b, h, qi, ki), 2 * heads + h)),
            pl.BlockSpec((blk, hv), lambda b, h, qi, ki, *_: (rows_q(b, h, qi, ki), 3 * heads + h)),
            vec(DIFF_HEAD_DIM), vec(DIFF_HEAD_DIM), vec(DIFF_HEAD_DIM), vec(DIFF_HEAD_DIM), vec(hv),
        ],
        out_specs=pl.BlockSpec((blk, hv), lambda b, h, qi, ki, *_: (rows_q(b, h, qi, ki), h)),
        scratch_shapes=[pltpu.VMEM((3, blk, blk), F32),
                        pltpu.VMEM((2, blk, 1), F32),
                        pltpu.VMEM((2, blk, 1), F32),
                        pltpu.VMEM((2, blk, hv), F32)],
    )
    row = lambda a: a.reshape(1, -1).astype(F32)
    return pl.pallas_call(
        functools.partial(_diff_attn_body, blk=blk, nk=nb, lam_init=lam_init),
        grid_spec=grid_spec,
        out_shape=jax.ShapeDtypeStruct((batch * seq, heads * hv), BF16),
        compiler_params=_params(4),
        name="diff_attention",
    )(slopes, proj, proj, proj, proj, row(lam_q1), row(lam_k1), row(lam_q2), row(lam_k2), row(g_subln))


def _chan_dft_body(u_ref, c_ref, s_ref, o_ref, *, gd):
    for g in range(FOURIER_GROUPS):
        lanes = slice(g * gd, (g + 1) * gd)
        ug = u_ref[:, lanes]
        o_ref[0, :, lanes] = jnp.dot(ug, c_ref[...], preferred_element_type=F32).astype(o_ref.dtype)
        o_ref[1, :, lanes] = jnp.dot(ug, s_ref[...], preferred_element_type=F32).astype(o_ref.dtype)


def _chan_dft(proj, c_tbl, s_tbl, *, batch, seq, fw, col_block, bm=1024):
    bm = min(bm, seq)
    nb = seq // bm
    gd = fw // FOURIER_GROUPS
    return pl.pallas_call(
        functools.partial(_chan_dft_body, gd=gd),
        grid=(batch, nb),
        in_specs=[pl.BlockSpec((bm, fw), lambda b, i: (b * nb + i, col_block)),
                  pl.BlockSpec((gd, gd), lambda b, i: (0, 0)),
                  pl.BlockSpec((gd, gd), lambda b, i: (0, 0))],
        out_specs=pl.BlockSpec((2, bm, fw), lambda b, i: (0, i, b)),
        out_shape=jax.ShapeDtypeStruct((2, seq, batch * fw), BF16),
        compiler_params=_params(2),
        name="channel_dft",
    )(proj, c_tbl, s_tbl)


def _fourier_out_body(f_ref, w_ref, z_ref, o_ref):
    y = jnp.dot(f_ref[...], w_ref[...], preferred_element_type=F32)
    o_ref[...] = (y * _silu(z_ref[...].astype(F32))).astype(o_ref.dtype)


def _fourier_out(f, w, proj, *, batch, seq, fw, gate_block, bm=1024):
    bm = min(bm, seq)
    nb = seq // bm
    return pl.pallas_call(
        _fourier_out_body,
        grid=(batch, nb),
        in_specs=[pl.BlockSpec((bm, fw), lambda b, i: (i, b)),
                  pl.BlockSpec((fw, fw), lambda b, i: (0, 0)),
                  pl.BlockSpec((bm, fw), lambda b, i: (b * nb + i, gate_block))],
        out_specs=pl.BlockSpec((bm, fw), lambda b, i: (b * nb + i, 0)),
        out_shape=jax.ShapeDtypeStruct((batch * seq, fw), BF16),
        compiler_params=_params(2),
        name="fourier_out",
    )(f, w, proj)


def _seq_dft_matrix(seq):
    hi = seq // DFT_SPLIT
    k = jnp.arange(seq, dtype=jnp.int32)
    a = jnp.arange(hi, dtype=jnp.int32) * DFT_SPLIT
    b = jnp.arange(DFT_SPLIT, dtype=jnp.int32)
    step = 2.0 * math.pi / seq
    ang_a = ((a[:, None] * k[None, :]) % seq).astype(F32) * step
    ang_b = ((b[:, None] * k[None, :]) % seq).astype(F32) * step
    ca, sa = jnp.cos(ang_a)[:, None, :], jnp.sin(ang_a)[:, None, :]
    cb, sb = jnp.cos(ang_b)[None, :, :], jnp.sin(ang_b)[None, :, :]
    cos = (ca * cb - sa * sb).reshape(seq, seq)
    sin = (sa * cb + ca * sb).reshape(seq, seq)
    return jnp.concatenate([cos, sin], axis=1).astype(BF16)


def _chan_dft_tables(gd, seq):
    n = jnp.arange(gd, dtype=jnp.int32)
    ang = ((n[:, None] * n[None, :]) % gd).astype(F32) * (2.0 * math.pi / gd)
    scale = 1.0 / math.sqrt(float(seq) * float(gd))
    return (jnp.cos(ang) * scale).astype(BF16), (-jnp.sin(ang) * scale).astype(BF16)


def _mem_attn_body(q_ref, z_ref, k_ref, v_ref, o_ref, *, hd):
    for h in range(MEM_HEADS):
        lanes = slice(h * hd, (h + 1) * hd)
        s = lax.dot_general(q_ref[:, lanes], k_ref[:, lanes], (((1,), (1,)), ((), ())),
                            preferred_element_type=F32)
        p = jnp.exp(s - jnp.max(s, axis=-1, keepdims=True))
        inv = 1.0 / jnp.sum(p, axis=-1, keepdims=True)
        o = jnp.dot(p.astype(BF16), v_ref[:, lanes], preferred_element_type=F32) * inv
        o_ref[:, lanes] = (o * _silu(z_ref[:, lanes].astype(F32))).astype(o_ref.dtype)


def _mem_attention(proj, km, vm, *, batch, seq, mem_len, mw, q_block, gate_block, bm=1024):
    bm = min(bm, seq)
    nb = seq // bm
    return pl.pallas_call(
        functools.partial(_mem_attn_body, hd=mw // MEM_HEADS),
        grid=(batch, nb),
        in_specs=[pl.BlockSpec((bm, mw), lambda b, i: (b * nb + i, q_block)),
                  pl.BlockSpec((bm, mw), lambda b, i: (b * nb + i, gate_block)),
                  pl.BlockSpec((mem_len, mw), lambda b, i: (b, 0)),
                  pl.BlockSpec((mem_len, mw), lambda b, i: (b, 0))],
        out_specs=pl.BlockSpec((bm, mw), lambda b, i: (b * nb + i, 0)),
        out_shape=jax.ShapeDtypeStruct((batch * seq, mw), BF16),
        compiler_params=_params(2),
        name="mem_attention",
    )(proj, proj, km, vm)


def _out_proj_body(yd_ref, yf_ref, ym_ref, w_ref, x_ref, o_ref, *, dw, fw):
    acc = jnp.dot(yd_ref[...], w_ref[0:dw, :], preferred_element_type=F32)
    acc += jnp.dot(yf_ref[...], w_ref[dw:dw + fw, :], preferred_element_type=F32)
    acc += jnp.dot(ym_ref[...], w_ref[dw + fw:, :], preferred_element_type=F32)
    o_ref[...] = x_ref[...] + acc


def _out_proj(yd, yf, ym, w, x, bm=1024, bn=512):
    m, d = x.shape
    dw, fw, mw = yd.shape[1], yf.shape[1], ym.shape[1]
    bm, bn = min(bm, m), min(bn, d)
    return pl.pallas_call(
        functools.partial(_out_proj_body, dw=dw, fw=fw),
        grid=(m // bm, d // bn),
        in_specs=[pl.BlockSpec((bm, dw), lambda i, j: (i, 0)),
                  pl.BlockSpec((bm, fw), lambda i, j: (i, 0)),
                  pl.BlockSpec((bm, mw), lambda i, j: (i, 0)),
                  pl.BlockSpec((dw + fw + mw, bn), lambda i, j: (0, j)),
                  pl.BlockSpec((bm, bn), lambda i, j: (i, j))],
        out_specs=pl.BlockSpec((bm, bn), lambda i, j: (i, j)),
        out_shape=jax.ShapeDtypeStruct((m, d), F32),
        compiler_params=_params(2),
        name="out_proj",
    )(yd, yf, ym, w, x)


def kernel(x, mem, g_in, w_in, g_mem, w_mem_k, w_mem_v, lam_q1, lam_k1, lam_q2, lam_k2, g_subln, w_fourier,
           w_out, g_final):
    batch, seq, d = x.shape
    mem_len = mem.shape[1]
    depth = w_in.shape[0]
    dw, fw = d // 2, d // 4
    mw = d - dw - fw
    heads = dw // DIFF_V_DIM
    assert fw == mw and dw == 2 * fw and dw % DIFF_V_DIM == 0 and seq % DFT_SPLIT == 0
    in_cols = 4 * dw + 2 * fw + 2 * mw
    assert w_in.shape[2] == in_cols

    col_scale = jnp.ones((in_cols,), F32)
    col_scale = col_scale.at[:dw].set(DIFF_HEAD_DIM ** -0.5)
    col_scale = col_scale.at[4 * dw + 2 * fw:4 * dw + 2 * fw + mw].set((mw // MEM_HEADS) ** -0.5)

    slopes = jnp.asarray([2.0 ** (-8.0 * (h + 1) / heads) for h in range(heads)], dtype=F32)
    seq_dft = _seq_dft_matrix(seq)
    c_tbl, s_tbl = _chan_dft_tables(fw // FOURIER_GROUPS, seq)
    mem2 = mem.reshape(batch * mem_len, d)

    h = x.reshape(batch * seq, d)
    for l in range(depth):
        lam_init = 0.8 - 0.6 * math.exp(-0.3 * l)
        w_in_l = (w_in[l] * col_scale[None, :]).astype(BF16)

        hn = _rmsnorm(h, g_in[l], BF16)
        proj = _matmul(hn, w_in_l, BF16, name="in_proj")

        y_d = _diff_attention(proj, slopes, lam_q1[l], lam_k1[l], lam_q2[l], lam_k2[l], g_subln[l],
                              batch=batch, seq=seq, heads=heads, lam_init=lam_init)

        g2 = _chan_dft(proj, c_tbl, s_tbl, batch=batch, seq=seq, fw=fw, col_block=(4 * dw) // fw)
        f = _matmul_acc(seq_dft, g2.reshape(2 * seq, batch * fw), BF16, name="seq_dft")
        y_f = _fourier_out(f, w_fourier[l].astype(BF16), proj, batch=batch, seq=seq, fw=fw,
                           gate_block=(4 * dw + fw) // fw)

        mn = _rmsnorm(mem2, g_mem[l], BF16)
        km = _matmul(mn, w_mem_k[l].astype(BF16), BF16, bm=512, bn=512, name="mem_k")
        vm = _matmul(mn, w_mem_v[l].astype(BF16), BF16, bm=512, bn=512, name="mem_v")
        y_m = _mem_attention(proj, km, vm, batch=batch, seq=seq, mem_len=mem_len, mw=mw,
                             q_block=(4 * dw + 2 * fw) // mw, gate_block=(4 * dw + 2 * fw + mw) // mw)

        h = _out_proj(y_d, y_f, y_m, w_out[l].astype(BF16), h)

    return _rmsnorm(h, g_final, x.dtype).reshape(batch, seq, d)
```

```python
import functools
import math

import jax
import jax.numpy as jnp
from jax import lax
from jax.experimental import pallas as pl
from jax.experimental.pallas import tpu as pltpu

RMS_EPS = 1e-6
DIFF_HEAD_DIM = 128
DIFF_V_DIM = 2 * DIFF_HEAD_DIM
FOURIER_GROUPS = 4
MEM_HEADS = 4
DFT_SPLIT = 128
LOG2E = math.log2(math.e)
ATTN_BLOCK = 1024
ATTN_STRIP = 256
SUB_ROWS = 32
LANES = 128

V7X_VMEM_BYTES = 64 * 1024 * 1024
VMEM_LIMIT_BYTES = V7X_VMEM_BYTES - 8 * 1024 * 1024

F32 = jnp.float32
BF16 = jnp.bfloat16


def _params(n_grid_dims, flags=None):
    return pltpu.CompilerParams(dimension_semantics=("arbitrary",) * n_grid_dims,
                                vmem_limit_bytes=VMEM_LIMIT_BYTES, flags=flags)


def _silu(z):
    return z / (1.0 + jnp.exp(-z))


def _rmsnorm_body(x_ref, g_ref, o_ref):
    x = x_ref[...].astype(F32)
    y = x * lax.rsqrt(jnp.mean(x * x, axis=-1, keepdims=True) + RMS_EPS)
    o_ref[...] = (y * g_ref[...]).astype(o_ref.dtype)


def _rmsnorm(x, g, out_dtype, bm=256):
    m, d = x.shape
    bm = min(bm, m)
    return pl.pallas_call(
        _rmsnorm_body,
        grid=(m // bm,),
        in_specs=[pl.BlockSpec((bm, d), lambda i: (i, 0)),
                  pl.BlockSpec((1, d), lambda i: (0, 0))],
        out_specs=pl.BlockSpec((bm, d), lambda i: (i, 0)),
        out_shape=jax.ShapeDtypeStruct((m, d), out_dtype),
        compiler_params=_params(1),
        name="rmsnorm",
    )(x, g.reshape(1, d).astype(F32))


def _matmul_body(a_ref, w_ref, o_ref):
    o_ref[...] = jnp.dot(a_ref[...], w_ref[...], preferred_element_type=F32).astype(o_ref.dtype)


def _matmul(a, w, out_dtype, bm=1024, bn=1024, name="matmul"):
    m, k = a.shape
    n = w.shape[1]
    bm, bn = min(bm, m), min(bn, n)
    return pl.pallas_call(
        _matmul_body,
        grid=(m // bm, n // bn),
        in_specs=[pl.BlockSpec((bm, k), lambda i, j: (i, 0)),
                  pl.BlockSpec((k, bn), lambda i, j: (0, j))],
        out_specs=pl.BlockSpec((bm, bn), lambda i, j: (i, j)),
        out_shape=jax.ShapeDtypeStruct((m, n), out_dtype),
        compiler_params=_params(2),
        name=name,
    )(a, w)


def _matmul_acc_body(a_ref, b_ref, o_ref, acc_ref, *, nk):
    kk = pl.program_id(2)

    @pl.when(kk == 0)
    def _zero():
        acc_ref[...] = jnp.zeros_like(acc_ref)

    acc_ref[...] += jnp.dot(a_ref[...], b_ref[...], preferred_element_type=F32)

    @pl.when(kk == nk - 1)
    def _store():
        o_ref[...] = acc_ref[...].astype(o_ref.dtype)


def _matmul_acc(a, b, out_dtype, bm=1024, bn=1024, bk=2048, name="matmul_acc"):
    m, k = a.shape
    n = b.shape[1]
    bm, bn, bk = min(bm, m), min(bn, n), min(bk, k)
    nk = k // bk
    return pl.pallas_call(
        functools.partial(_matmul_acc_body, nk=nk),
        grid=(m // bm, n // bn, nk),
        in_specs=[pl.BlockSpec((bm, bk), lambda i, j, kk: (i, kk)),
                  pl.BlockSpec((bk, bn), lambda i, j, kk: (kk, j))],
        out_specs=pl.BlockSpec((bm, bn), lambda i, j, kk: (i, j)),
        out_shape=jax.ShapeDtypeStruct((m, n), out_dtype),
        scratch_shapes=[pltpu.VMEM((bm, bn), F32)],
        compiler_params=_params(3),
        name=name,
    )(a, b)


def _diff_attn_body(slopes_ref, q_ref, k_ref, v_ref, z_ref, lq1_ref, lk1_ref, lq2_ref, lk2_ref, g_ref,
                    o_ref, tbl_ref, t0_ref, t1_ref, p0_ref, p1_ref, pm_ref, ps_ref, msub_ref, m_ref, l_ref,
                    acc_ref, *, blk, strip, nk, lam_init):
    h = pl.program_id(1)
    qi = pl.program_id(2)
    ki = pl.program_id(3)
    slope2 = slopes_ref[h] * LOG2E

    @pl.when((qi == 0) & (ki == 0))
    def _fill_tables():
        row = lax.broadcasted_iota(jnp.int32, (blk, blk), 0)
        col = lax.broadcasted_iota(jnp.int32, (blk, blk), 1)
        d = (row - col).astype(F32) * slope2
        tbl_ref[0] = d
        tbl_ref[1] = -d
        tbl_ref[2] = jnp.abs(d)

    @pl.when(ki == 0)
    def _reset_state():
        m_ref[...] = jnp.full_like(m_ref, -jnp.inf)
        l_ref[...] = jnp.zeros_like(l_ref)
        acc_ref[...] = jnp.zeros_like(acc_ref)

    sel = jnp.where(ki < qi, 0, jnp.where(ki > qi, 1, 2))
    c = slope2 * (jnp.abs(qi - ki) * blk).astype(F32)
    slot = jnp.minimum(ki, 0)
    n_chunks = blk // LANES
    for r in range(blk // strip):
        rows = slice(r * strip, (r + 1) * strip)
        for cm, t_ref, p_ref in ((0, t0_ref, p0_ref), (1, t1_ref, p1_ref)):
            lanes = slice(cm * DIFF_HEAD_DIM, (cm + 1) * DIFF_HEAD_DIM)
            t_ref[slot, rows, :] = lax.dot_general(
                q_ref[rows, lanes], k_ref[:, lanes], (((1,), (1,)), ((), ())),
                preferred_element_type=F32) - tbl_ref[sel, rows, :]

            for s in range(strip // SUB_ROWS):
                srows = slice(r * strip + s * SUB_ROWS, r * strip + (s + 1) * SUB_ROWS)
                pm = t_ref[slot, srows, 0:LANES]
                for j in range(1, n_chunks):
                    pm = jnp.maximum(pm, t_ref[slot, srows, j * LANES:(j + 1) * LANES])
                pm_ref[cm, srows, :] = pm

            m_old = m_ref[cm, rows, :]
            m_new = jnp.maximum(m_old, jnp.max(pm_ref[cm, rows, :], axis=-1, keepdims=True) - c)
            alpha = jnp.exp2(m_old - m_new)
            m_ref[cm, rows, :] = m_new
            msub_ref[cm, rows, :] = jnp.broadcast_to(m_new + c, (strip, LANES))

            for s in range(strip // SUB_ROWS):
                srows = slice(r * strip + s * SUB_ROWS, r * strip + (s + 1) * SUB_ROWS)
                msub = msub_ref[cm, srows, :]
                ps = None
                for j in range(n_chunks):
                    chunk = slice(j * LANES, (j + 1) * LANES)
                    p = jnp.exp2(t_ref[slot, srows, chunk] - msub)
                    ps = p if ps is None else ps + p
                    p_ref[slot, srows, chunk] = p.astype(BF16)
                ps_ref[cm, srows, :] = ps

            l_ref[cm, rows, :] = alpha * l_ref[cm, rows, :] + jnp.sum(ps_ref[cm, rows, :], axis=-1,
                                                                     keepdims=True)
            acc_ref[cm, rows, :] = alpha * acc_ref[cm, rows, :] + jnp.dot(
                p_ref[slot, rows, :], v_ref[...], preferred_element_type=F32)

    @pl.when(ki == nk - 1)
    def _finish():
        lam = (jnp.exp(jnp.sum(lq1_ref[...] * lk1_ref[...], axis=-1, keepdims=True))
               - jnp.exp(jnp.sum(lq2_ref[...] * lk2_ref[...], axis=-1, keepdims=True)) + lam_init)
        o = acc_ref[0] * (1.0 / l_ref[0]) - lam * (acc_ref[1] * (1.0 / l_ref[1]))
        o = o * lax.rsqrt(jnp.mean(o * o, axis=-1, keepdims=True) + RMS_EPS)
        o = o * g_ref[...] * (1.0 - lam_init)
        o_ref[...] = (o * _silu(z_ref[...].astype(F32))).astype(o_ref.dtype)


def _diff_attention(proj, slopes, lam_q1, lam_k1, lam_q2, lam_k2, g_subln, *, batch, seq, heads, lam_init,
                    blk=None):
    blk = min(blk or ATTN_BLOCK, seq)
    nb = seq // blk
    hv = DIFF_V_DIM

    def rows_q(b, h, qi, ki, *_):
        return b * nb + qi

    def rows_k(b, h, qi, ki, *_):
        return b * nb + ki

    vec = lambda n: pl.BlockSpec((1, n), lambda b, h, qi, ki, *_: (0, 0))
    grid_spec = pltpu.PrefetchScalarGridSpec(
        num_scalar_prefetch=1,
        grid=(batch, heads, nb, nb),
        in_specs=[
            pl.BlockSpec((blk, hv), lambda b, h, qi, ki, *_: (rows_q(b, h, qi, ki), h)),
            pl.BlockSpec((blk, hv), lambda b, h, qi, ki, *_: (rows_k(b, h, qi, ki), heads + h)),
            pl.BlockSpec((blk, hv), lambda b, h, qi, ki, *_: (rows_k(b, h, qi, ki), 2 * heads + h)),
            pl.BlockSpec((blk, hv), lambda b, h, qi, ki, *_: (rows_q(b, h, qi, ki), 3 * heads + h)),
            vec(DIFF_HEAD_DIM), vec(DIFF_HEAD_DIM), vec(DIFF_HEAD_DIM), vec(DIFF_HEAD_DIM), vec(hv),
        ],
        out_specs=pl.BlockSpec((blk, hv), lambda b, h, qi, ki, *_: (rows_q(b, h, qi, ki), h)),
        scratch_shapes=[pltpu.VMEM((3, blk, blk), F32),
                        pltpu.VMEM((1, blk, blk), F32),
                        pltpu.VMEM((1, blk, blk), F32),
                        pltpu.VMEM((1, blk, blk), BF16),
                        pltpu.VMEM((1, blk, blk), BF16),
                        pltpu.VMEM((2, blk, LANES), F32),
                        pltpu.VMEM((2, blk, LANES), F32),
                        pltpu.VMEM((2, blk, LANES), F32),
                        pltpu.VMEM((2, blk, 1), F32),
                        pltpu.VMEM((2, blk, 1), F32),
                        pltpu.VMEM((2, blk, hv), F32)],
    )
    row = lambda a: a.reshape(1, -1).astype(F32)
    return pl.pallas_call(
        functools.partial(_diff_attn_body, blk=blk, strip=min(ATTN_STRIP, blk), nk=nb, lam_init=lam_init),
        grid_spec=grid_spec,
        out_shape=jax.ShapeDtypeStruct((batch * seq, heads * hv), BF16),
        compiler_params=_params(4),
        name="diff_attention",
    )(slopes, proj, proj, proj, proj, row(lam_q1), row(lam_k1), row(lam_q2), row(lam_k2), row(g_subln))


def _chan_dft_body(u_ref, c_ref, s_ref, o_ref, *, gd):
    for g in range(FOURIER_GROUPS):
        lanes = slice(g * gd, (g + 1) * gd)
        ug = u_ref[:, lanes]
        o_ref[0, :, lanes] = jnp.dot(ug, c_ref[...], preferred_element_type=F32).astype(o_ref.dtype)
        o_ref[1, :, lanes] = jnp.dot(ug, s_ref[...], preferred_element_type=F32).astype(o_ref.dtype)


def _chan_dft(proj, c_tbl, s_tbl, *, batch, seq, fw, col_block, bm=1024):
    bm = min(bm, seq)
    nb = seq // bm
    gd = fw // FOURIER_GROUPS
    return pl.pallas_call(
        functools.partial(_chan_dft_body, gd=gd),
        grid=(batch, nb),
        in_specs=[pl.BlockSpec((bm, fw), lambda b, i: (b * nb + i, col_block)),
                  pl.BlockSpec((gd, gd), lambda b, i: (0, 0)),
                  pl.BlockSpec((gd, gd), lambda b, i: (0, 0))],
        out_specs=pl.BlockSpec((2, bm, fw), lambda b, i: (0, i, b)),
        out_shape=jax.ShapeDtypeStruct((2, seq, batch * fw), BF16),
        compiler_params=_params(2),
        name="channel_dft",
    )(proj, c_tbl, s_tbl)


def _fourier_out_body(f_ref, w_ref, z_ref, o_ref):
    y = jnp.dot(f_ref[...], w_ref[...], preferred_element_type=F32)
    o_ref[...] = (y * _silu(z_ref[...].astype(F32))).astype(o_ref.dtype)


def _fourier_out(f, w, proj, *, batch, seq, fw, gate_block, bm=1024):
    bm = min(bm, seq)
    nb = seq // bm
    return pl.pallas_call(
        _fourier_out_body,
        grid=(batch, nb),
        in_specs=[pl.BlockSpec((bm, fw), lambda b, i: (i, b)),
                  pl.BlockSpec((fw, fw), lambda b, i: (0, 0)),
                  pl.BlockSpec((bm, fw), lambda b, i: (b * nb + i, gate_block))],
        out_specs=pl.BlockSpec((bm, fw), lambda b, i: (b * nb + i, 0)),
        out_shape=jax.ShapeDtypeStruct((batch * seq, fw), BF16),
        compiler_params=_params(2),
        name="fourier_out",
    )(f, w, proj)


def _seq_dft_matrix(seq):
    hi = seq // DFT_SPLIT
    k = jnp.arange(seq, dtype=jnp.int32)
    a = jnp.arange(hi, dtype=jnp.int32) * DFT_SPLIT
    b = jnp.arange(DFT_SPLIT, dtype=jnp.int32)
    step = 2.0 * math.pi / seq
    ang_a = ((a[:, None] * k[None, :]) % seq).astype(F32) * step
    ang_b = ((b[:, None] * k[None, :]) % seq).astype(F32) * step
    ca, sa = jnp.cos(ang_a)[:, None, :], jnp.sin(ang_a)[:, None, :]
    cb, sb = jnp.cos(ang_b)[None, :, :], jnp.sin(ang_b)[None, :, :]
    cos = (ca * cb - sa * sb).reshape(seq, seq)
    sin = (sa * cb + ca * sb).reshape(seq, seq)
    return jnp.concatenate([cos, sin], axis=1).astype(BF16)


def _chan_dft_tables(gd, seq):
    n = jnp.arange(gd, dtype=jnp.int32)
    ang = ((n[:, None] * n[None, :]) % gd).astype(F32) * (2.0 * math.pi / gd)
    scale = 1.0 / math.sqrt(float(seq) * float(gd))
    return (jnp.cos(ang) * scale).astype(BF16), (-jnp.sin(ang) * scale).astype(BF16)


def _mem_attn_body(q_ref, z_ref, k_ref, v_ref, o_ref, *, hd):
    for h in range(MEM_HEADS):
        lanes = slice(h * hd, (h + 1) * hd)
        s = lax.dot_general(q_ref[:, lanes], k_ref[:, lanes], (((1,), (1,)), ((), ())),
                            preferred_element_type=F32)
        p = jnp.exp(s - jnp.max(s, axis=-1, keepdims=True))
        inv = 1.0 / jnp.sum(p, axis=-1, keepdims=True)
        o = jnp.dot(p.astype(BF16), v_ref[:, lanes], preferred_element_type=F32) * inv
        o_ref[:, lanes] = (o * _silu(z_ref[:, lanes].astype(F32))).astype(o_ref.dtype)


def _mem_attention(proj, km, vm, *, batch, seq, mem_len, mw, q_block, gate_block, bm=1024):
    bm = min(bm, seq)
    nb = seq // bm
    return pl.pallas_call(
        functools.partial(_mem_attn_body, hd=mw // MEM_HEADS),
        grid=(batch, nb),
        in_specs=[pl.BlockSpec((bm, mw), lambda b, i: (b * nb + i, q_block)),
                  pl.BlockSpec((bm, mw), lambda b, i: (b * nb + i, gate_block)),
                  pl.BlockSpec((mem_len, mw), lambda b, i: (b, 0)),
                  pl.BlockSpec((mem_len, mw), lambda b, i: (b, 0))],
        out_specs=pl.BlockSpec((bm, mw), lambda b, i: (b * nb + i, 0)),
        out_shape=jax.ShapeDtypeStruct((batch * seq, mw), BF16),
        compiler_params=_params(2),
        name="mem_attention",
    )(proj, proj, km, vm)


def _out_proj_body(yd_ref, yf_ref, ym_ref, w_ref, x_ref, o_ref, *, dw, fw):
    acc = jnp.dot(yd_ref[...], w_ref[0:dw, :], preferred_element_type=F32)
    acc += jnp.dot(yf_ref[...], w_ref[dw:dw + fw, :], preferred_element_type=F32)
    acc += jnp.dot(ym_ref[...], w_ref[dw + fw:, :], preferred_element_type=F32)
    o_ref[...] = x_ref[...] + acc


def _out_proj(yd, yf, ym, w, x, bm=1024, bn=512):
    m, d = x.shape
    dw, fw, mw = yd.shape[1], yf.shape[1], ym.shape[1]
    bm, bn = min(bm, m), min(bn, d)
    return pl.pallas_call(
        functools.partial(_out_proj_body, dw=dw, fw=fw),
        grid=(m // bm, d // bn),
        in_specs=[pl.BlockSpec((bm, dw), lambda i, j: (i, 0)),
                  pl.BlockSpec((bm, fw), lambda i, j: (i, 0)),
                  pl.BlockSpec((bm, mw), lambda i, j: (i, 0)),
                  pl.BlockSpec((dw + fw + mw, bn), lambda i, j: (0, j)),
                  pl.BlockSpec((bm, bn), lambda i, j: (i, j))],
        out_specs=pl.BlockSpec((bm, bn), lambda i, j: (i, j)),
        out_shape=jax.ShapeDtypeStruct((m, d), F32),
        compiler_params=_params(2),
        name="out_proj",
    )(yd, yf, ym, w, x)


def kernel(x, mem, g_in, w_in, g_mem, w_mem_k, w_mem_v, lam_q1, lam_k1, lam_q2, lam_k2, g_subln, w_fourier,
           w_out, g_final):
    batch, seq, d = x.shape
    mem_len = mem.shape[1]
    depth = w_in.shape[0]
    dw, fw = d // 2, d // 4
    mw = d - dw - fw
    heads = dw // DIFF_V_DIM
    assert fw == mw and dw == 2 * fw and dw % DIFF_V_DIM == 0 and seq % DFT_SPLIT == 0
    in_cols = 4 * dw + 2 * fw + 2 * mw
    assert w_in.shape[2] == in_cols

    col_scale = jnp.ones((in_cols,), F32)
    col_scale = col_scale.at[:dw].set(DIFF_HEAD_DIM ** -0.5 * LOG2E)
    col_scale = col_scale.at[4 * dw + 2 * fw:4 * dw + 2 * fw + mw].set((mw // MEM_HEADS) ** -0.5)

    slopes = jnp.asarray([2.0 ** (-8.0 * (h + 1) / heads) for h in range(heads)], dtype=F32)
    seq_dft = _seq_dft_matrix(seq)
    c_tbl, s_tbl = _chan_dft_tables(fw // FOURIER_GROUPS, seq)
    mem2 = mem.reshape(batch * mem_len, d)

    h = x.reshape(batch * seq, d)
    for l in range(depth):
        lam_init = 0.8 - 0.6 * math.exp(-0.3 * l)
        w_in_l = (w_in[l] * col_scale[None, :]).astype(BF16)

        hn = _rmsnorm(h, g_in[l], BF16)
        proj = _matmul(hn, w_in_l, BF16, name="in_proj")

        y_d = _diff_attention(proj, slopes, lam_q1[l], lam_k1[l], lam_q2[l], lam_k2[l], g_subln[l],
                              batch=batch, seq=seq, heads=heads, lam_init=lam_init)

        g2 = _chan_dft(proj, c_tbl, s_tbl, batch=batch, seq=seq, fw=fw, col_block=(4 * dw) // fw)
        f = _matmul_acc(seq_dft, g2.reshape(2 * seq, batch * fw), BF16, name="seq_dft")
        y_f = _fourier_out(f, w_fourier[l].astype(BF16), proj, batch=batch, seq=seq, fw=fw,
                           gate_block=(4 * dw + fw) // fw)

        mn = _rmsnorm(mem2, g_mem[l], BF16)
        km = _matmul(mn, w_mem_k[l].astype(BF16), BF16, bm=512, bn=512, name="mem_k")
        vm = _matmul(mn, w_mem_v[l].astype(BF16), BF16, bm=512, bn=512, name="mem_v")
        y_m = _mem_attention(proj, km, vm, batch=batch, seq=seq, mem_len=mem_len, mw=mw,
                             q_block=(4 * dw + 2 * fw) // mw, gate_block=(4 * dw + 2 * fw + mw) // mw)

        h = _out_proj(y_d, y_f, y_m, w_out[l].astype(BF16), h)

    return _rmsnorm(h, g_final, x.dtype).reshape(batch, seq, d)
```

```python
import functools
import math

import jax
import jax.numpy as jnp
from jax import lax
from jax.experimental import pallas as pl
from jax.experimental.pallas import tpu as pltpu

RMS_EPS = 1e-6
DIFF_HEAD_DIM = 128
DIFF_V_DIM = 2 * DIFF_HEAD_DIM
FOURIER_GROUPS = 4
MEM_HEADS = 4
DFT_SPLIT = 128
ATTN_BLOCK = 1024
ATTN_STRIP = 256
SUB_ROWS = 32
LANES = 128
FEATURE_SPLIT = 32
FAST_LIMIT = 60.0
BOUND_SLACK = 1.0 + 2.0 ** -7

V7X_VMEM_BYTES = 64 * 1024 * 1024
VMEM_LIMIT_BYTES = V7X_VMEM_BYTES - 8 * 1024 * 1024

F32 = jnp.float32
BF16 = jnp.bfloat16


def _params(n_grid_dims):
    return pltpu.CompilerParams(dimension_semantics=("arbitrary",) * n_grid_dims,
                                vmem_limit_bytes=VMEM_LIMIT_BYTES)


def _silu(z):
    return z / (1.0 + jnp.exp(-z))


def _rmsnorm_body(x_ref, g_ref, o_ref):
    x = x_ref[...].astype(F32)
    y = x * lax.rsqrt(jnp.mean(x * x, axis=-1, keepdims=True) + RMS_EPS)
    o_ref[...] = (y * g_ref[...]).astype(o_ref.dtype)


def _rmsnorm(x, g, out_dtype, bm=256):
    m, d = x.shape
    bm = min(bm, m)
    return pl.pallas_call(
        _rmsnorm_body,
        grid=(m // bm,),
        in_specs=[pl.BlockSpec((bm, d), lambda i: (i, 0)),
                  pl.BlockSpec((1, d), lambda i: (0, 0))],
        out_specs=pl.BlockSpec((bm, d), lambda i: (i, 0)),
        out_shape=jax.ShapeDtypeStruct((m, d), out_dtype),
        compiler_params=_params(1),
        name="rmsnorm",
    )(x, g.reshape(1, d).astype(F32))


def _matmul_body(a_ref, w_ref, o_ref):
    o_ref[...] = jnp.dot(a_ref[...], w_ref[...], preferred_element_type=F32).astype(o_ref.dtype)


def _matmul(a, w, out_dtype, bm=1024, bn=1024, name="matmul"):
    m, k = a.shape
    n = w.shape[1]
    bm, bn = min(bm, m), min(bn, n)
    return pl.pallas_call(
        _matmul_body,
        grid=(m // bm, n // bn),
        in_specs=[pl.BlockSpec((bm, k), lambda i, j: (i, 0)),
                  pl.BlockSpec((k, bn), lambda i, j: (0, j))],
        out_specs=pl.BlockSpec((bm, bn), lambda i, j: (i, j)),
        out_shape=jax.ShapeDtypeStruct((m, n), out_dtype),
        compiler_params=_params(2),
        name=name,
    )(a, w)


def _matmul_acc_body(a_ref, b_ref, o_ref, acc_ref, *, nk):
    kk = pl.program_id(2)

    @pl.when(kk == 0)
    def _zero():
        acc_ref[...] = jnp.zeros_like(acc_ref)

    acc_ref[...] += jnp.dot(a_ref[...], b_ref[...], preferred_element_type=F32)

    @pl.when(kk == nk - 1)
    def _store():
        o_ref[...] = acc_ref[...].astype(o_ref.dtype)


def _matmul_acc(a, b, out_dtype, bm=1024, bn=1024, bk=2048, name="matmul_acc"):
    m, k = a.shape
    n = b.shape[1]
    bm, bn, bk = min(bm, m), min(bn, n), min(bk, k)
    nk = k // bk
    return pl.pallas_call(
        functools.partial(_matmul_acc_body, nk=nk),
        grid=(m // bm, n // bn, nk),
        in_specs=[pl.BlockSpec((bm, bk), lambda i, j, kk: (i, kk)),
                  pl.BlockSpec((bk, bn), lambda i, j, kk: (kk, j))],
        out_specs=pl.BlockSpec((bm, bn), lambda i, j, kk: (i, j)),
        out_shape=jax.ShapeDtypeStruct((m, n), out_dtype),
        scratch_shapes=[pltpu.VMEM((bm, bn), F32)],
        compiler_params=_params(3),
        name=name,
    )(a, b)


def _row_norm_body(qk_ref, o_ref, *, groups):
    lane = lax.broadcasted_iota(jnp.int32, (1, LANES), 1)
    out = jnp.zeros((1, LANES), F32)
    for g in range(groups):
        xg = qk_ref[:, g * DIFF_HEAD_DIM:(g + 1) * DIFF_HEAD_DIM].astype(F32)
        n2 = jnp.max(jnp.sum(xg * xg, axis=-1, keepdims=True), axis=0, keepdims=True)
        out = jnp.where(lane == g, n2, out)
    o_ref[0] = out


def _softmax_offsets(proj, *, batch, seq, heads, blk):
    nb = seq // blk
    g = 2 * heads
    assert 2 * g <= LANES
    part = pl.pallas_call(
        functools.partial(_row_norm_body, groups=2 * g),
        grid=(batch * nb,),
        in_specs=[pl.BlockSpec((blk, 2 * g * DIFF_HEAD_DIM), lambda i: (i, 0))],
        out_specs=pl.BlockSpec((1, 1, LANES), lambda i: (i, 0, 0)),
        out_shape=jax.ShapeDtypeStruct((batch * nb, 1, LANES), F32),
        compiler_params=_params(1),
        name="row_norms",
    )(proj)
    norms = jnp.sqrt(part.reshape(batch, nb, LANES))
    qmax = norms[:, :, :g]
    kmax = jnp.max(norms[:, :, g:2 * g], axis=1)
    bound = qmax * kmax[:, None, :] * (BOUND_SLACK * BOUND_SLACK)
    safe = jnp.all((2.0 * bound <= FAST_LIMIT).reshape(batch, nb, heads, 2), axis=-1)
    flags = jnp.transpose(safe, (0, 2, 1)).astype(jnp.int32).reshape(batch * heads * nb)
    return kmax.reshape(batch * g), flags


def _diff_attn_body(slopes_ref, kmax_ref, fast_ref, q_ref, k_ref, v_ref, z_ref, lq1_ref, lk1_ref, lq2_ref,
                    lk2_ref, g_ref, o_ref, tbl_ref, t0_ref, t1_ref, p0_ref, p1_ref, pm_ref, msub_ref,
                    qaug_ref, kaug_ref, kbase_ref, m_ref, lp_ref, acc_ref,
                    *, blk, strip, nk, heads, lam_init, exact_slopes):
    b = pl.program_id(0)
    h = pl.program_id(1)
    qi = pl.program_id(2)
    ki = pl.program_id(3)
    slope = slopes_ref[h]
    n_chunks = blk // LANES
    lane = lax.broadcasted_iota(jnp.int32, (blk, LANES), 1)
    row = lax.broadcasted_iota(jnp.int32, (blk, LANES), 0)
    row_lo = (row & (FEATURE_SPLIT - 1)).astype(F32)
    row_hi = (row & ~(FEATURE_SPLIT - 1)).astype(F32)

    @pl.when((qi == 0) & (ki == 0))
    def _fill_tables():
        r2 = lax.broadcasted_iota(jnp.int32, (blk, blk), 0)
        c2 = lax.broadcasted_iota(jnp.int32, (blk, blk), 1)
        d = (r2 - c2).astype(F32) * slope
        tbl_ref[0] = d
        tbl_ref[1] = -d
        tbl_ref[2] = jnp.abs(d)
        for sel, sign in ((0, -1.0), (1, 1.0), (2, 0.0)):
            s = sign * slope
            kb = jnp.where(lane <= 1, s,
                           jnp.where(lane == 2, -s * row_lo,
                                     jnp.where(lane == 3, -s * row_hi,
                                               jnp.where(lane == 4, 1.0, 0.0))))
            kbase_ref[sel] = kb

    @pl.when(ki == 0)
    def _reset_state():
        m_ref[...] = jnp.full_like(m_ref, -jnp.inf)
        lp_ref[...] = jnp.zeros_like(lp_ref)
        acc_ref[...] = jnp.zeros_like(acc_ref)
        for cm in range(2):
            lanes = slice(cm * DIFF_HEAD_DIM, (cm + 1) * DIFF_HEAD_DIM)
            qf = q_ref[:, lanes].astype(F32)
            qn = jnp.sqrt(jnp.sum(qf * qf, axis=-1, keepdims=True))
            bound = qn * (kmax_ref[(b * heads + h) * 2 + cm] * BOUND_SLACK)
            m_fix = jnp.broadcast_to(bound, (blk, LANES)).astype(BF16).astype(F32)
            qfeat = jnp.where(lane == 0, row_lo,
                              jnp.where(lane == 1, row_hi,
                                        jnp.where(lane == 4, -m_fix,
                                                  jnp.where((lane == 2) | (lane == 3) | (lane == 5), 1.0, 0.0))))
            qaug_ref[cm, :, 0:DIFF_HEAD_DIM] = q_ref[:, lanes]
            qaug_ref[cm, :, DIFF_HEAD_DIM:] = qfeat.astype(BF16)

    sel = jnp.where(ki < qi, 0, jnp.where(ki > qi, 1, 2))
    c = slope * (jnp.abs(qi - ki) * blk).astype(F32)
    fast = (fast_ref[(b * heads + h) * nk + qi] == 1) & exact_slopes

    def _fast_step(diag):
        kfeat = jnp.where(lane == 5, -c, kbase_ref[sel]).astype(BF16)
        for cm in range(2):
            lanes = slice(cm * DIFF_HEAD_DIM, (cm + 1) * DIFF_HEAD_DIM)
            kaug_ref[cm, :, 0:DIFF_HEAD_DIM] = k_ref[:, lanes]
            kaug_ref[cm, :, DIFF_HEAD_DIM:] = kfeat
        for r in range(blk // strip):
            rows = slice(r * strip, (r + 1) * strip)
            for cm in range(2):
                x = lax.dot_general(qaug_ref[cm, rows, :], kaug_ref[cm], (((1,), (1,)), ((), ())),
                                    preferred_element_type=F32)
                if diag:
                    x = x - tbl_ref[2, rows, :]
                p = jnp.exp(x)
                ps = p[:, 0:LANES]
                for j in range(1, n_chunks):
                    ps = ps + p[:, j * LANES:(j + 1) * LANES]
                lp_ref[cm, rows, :] += ps
                acc_ref[cm, rows, :] += jnp.dot(p.astype(BF16), v_ref[...], preferred_element_type=F32)

    @pl.when(fast & (ki != qi))
    def _fast_off_diagonal():
        _fast_step(False)

    @pl.when(fast & (ki == qi))
    def _fast_diagonal():
        _fast_step(True)

    @pl.when(jnp.logical_not(fast))
    def _general_step():
        slot = jnp.minimum(ki, 0)
        for r in range(blk // strip):
            rows = slice(r * strip, (r + 1) * strip)
            for cm, t_ref, p_ref in ((0, t0_ref, p0_ref), (1, t1_ref, p1_ref)):
                lanes = slice(cm * DIFF_HEAD_DIM, (cm + 1) * DIFF_HEAD_DIM)
                t_ref[slot, rows, :] = lax.dot_general(
                    q_ref[rows, lanes], k_ref[:, lanes], (((1,), (1,)), ((), ())),
                    preferred_element_type=F32) - tbl_ref[sel, rows, :]

                for s in range(strip // SUB_ROWS):
                    srows = slice(r * strip + s * SUB_ROWS, r * strip + (s + 1) * SUB_ROWS)
                    pm = t_ref[slot, srows, 0:LANES]
                    for j in range(1, n_chunks):
                        pm = jnp.maximum(pm, t_ref[slot, srows, j * LANES:(j + 1) * LANES])
                    pm_ref[cm, srows, :] = pm

                m_old = m_ref[cm, rows, :]
                m_new = jnp.maximum(m_old, jnp.max(pm_ref[cm, rows, :], axis=-1, keepdims=True) - c)
                alpha = jnp.exp(m_old - m_new)
                m_ref[cm, rows, :] = m_new
                msub_ref[cm, rows, :] = jnp.broadcast_to(m_new + c, (strip, LANES))

                for s in range(strip // SUB_ROWS):
                    srows = slice(r * strip + s * SUB_ROWS, r * strip + (s + 1) * SUB_ROWS)
                    msub = msub_ref[cm, srows, :]
                    ps = None
                    for j in range(n_chunks):
                        chunk = slice(j * LANES, (j + 1) * LANES)
                        p = jnp.exp(t_ref[slot, srows, chunk] - msub)
                        ps = p if ps is None else ps + p
                        p_ref[slot, srows, chunk] = p.astype(BF16)
                    pm_ref[cm, srows, :] = ps

                lp_ref[cm, rows, :] = alpha * lp_ref[cm, rows, :] + pm_ref[cm, rows, :]
                acc_ref[cm, rows, :] = alpha * acc_ref[cm, rows, :] + jnp.dot(
                    p_ref[slot, rows, :], v_ref[...], preferred_element_type=F32)

    @pl.when(ki == nk - 1)
    def _finish():
        lam = (jnp.exp(jnp.sum(lq1_ref[...] * lk1_ref[...], axis=-1, keepdims=True))
               - jnp.exp(jnp.sum(lq2_ref[...] * lk2_ref[...], axis=-1, keepdims=True)) + lam_init)
        l0 = jnp.sum(lp_ref[0], axis=-1, keepdims=True)
        l1 = jnp.sum(lp_ref[1], axis=-1, keepdims=True)
        o = acc_ref[0] * (1.0 / l0) - lam * (acc_ref[1] * (1.0 / l1))
        o = o * lax.rsqrt(jnp.mean(o * o, axis=-1, keepdims=True) + RMS_EPS)
        o = o * g_ref[...] * (1.0 - lam_init)
        o_ref[...] = (o * _silu(z_ref[...].astype(F32))).astype(o_ref.dtype)


def _diff_attention(proj, slope_values, lam_q1, lam_k1, lam_q2, lam_k2, g_subln, *, batch, seq, heads,
                    lam_init, blk=None):
    blk = min(blk or ATTN_BLOCK, seq)
    nb = seq // blk
    hv = DIFF_V_DIM
    exact_slopes = all(math.frexp(s)[0] == 0.5 for s in slope_values) and nb <= 256 and blk <= 8192
    slopes = jnp.asarray(slope_values, dtype=F32)
    kmax, fast_flags = _softmax_offsets(proj, batch=batch, seq=seq, heads=heads, blk=blk)

    def rows_q(b, h, qi, ki, *_):
        return b * nb + qi

    def rows_k(b, h, qi, ki, *_):
        return b * nb + ki

    vec = lambda n: pl.BlockSpec((1, n), lambda b, h, qi, ki, *_: (0, 0))
    grid_spec = pltpu.PrefetchScalarGridSpec(
        num_scalar_prefetch=3,
        grid=(batch, heads, nb, nb),
        in_specs=[
            pl.BlockSpec((blk, hv), lambda b, h, qi, ki, *_: (rows_q(b, h, qi, ki), h)),
            pl.BlockSpec((blk, hv), lambda b, h, qi, ki, *_: (rows_k(b, h, qi, ki), heads + h)),
            pl.BlockSpec((blk, hv), lambda b, h, qi, ki, *_: (rows_k(b, h, qi, ki), 2 * heads + h)),
            pl.BlockSpec((blk, hv), lambda b, h, qi, ki, *_: (rows_q(b, h, qi, ki), 3 * heads + h)),
            vec(DIFF_HEAD_DIM), vec(DIFF_HEAD_DIM), vec(DIFF_HEAD_DIM), vec(DIFF_HEAD_DIM), vec(hv),
        ],
        out_specs=pl.BlockSpec((blk, hv), lambda b, h, qi, ki, *_: (rows_q(b, h, qi, ki), h)),
        scratch_shapes=[pltpu.VMEM((3, blk, blk), F32),
                        pltpu.VMEM((1, blk, blk), F32),
                        pltpu.VMEM((1, blk, blk), F32),
                        pltpu.VMEM((1, blk, blk), BF16),
                        pltpu.VMEM((1, blk, blk), BF16),
                        pltpu.VMEM((2, blk, LANES), F32),
                        pltpu.VMEM((2, blk, LANES), F32),
                        pltpu.VMEM((2, blk, 2 * DIFF_HEAD_DIM), BF16),
                        pltpu.VMEM((2, blk, 2 * DIFF_HEAD_DIM), BF16),
                        pltpu.VMEM((3, blk, LANES), F32),
                        pltpu.VMEM((2, blk, 1), F32),
                        pltpu.VMEM((2, blk, LANES), F32),
                        pltpu.VMEM((2, blk, hv), F32)],
    )
    row = lambda a: a.reshape(1, -1).astype(F32)
    return pl.pallas_call(
        functools.partial(_diff_attn_body, blk=blk, strip=min(ATTN_STRIP, blk), nk=nb, heads=heads,
                          lam_init=lam_init, exact_slopes=exact_slopes),
        grid_spec=grid_spec,
        out_shape=jax.ShapeDtypeStruct((batch * seq, heads * hv), BF16),
        compiler_params=_params(4),
        name="diff_attention",
    )(slopes, kmax, fast_flags, proj, proj, proj, proj, row(lam_q1), row(lam_k1), row(lam_q2), row(lam_k2),
      row(g_subln))


def _chan_dft_body(u_ref, c_ref, s_ref, o_ref, *, gd):
    for g in range(FOURIER_GROUPS):
        lanes = slice(g * gd, (g + 1) * gd)
        ug = u_ref[:, lanes]
        o_ref[0, :, lanes] = jnp.dot(ug, c_ref[...], preferred_element_type=F32).astype(o_ref.dtype)
        o_ref[1, :, lanes] = jnp.dot(ug, s_ref[...], preferred_element_type=F32).astype(o_ref.dtype)


def _chan_dft(proj, c_tbl, s_tbl, *, batch, seq, fw, col_block, bm=1024):
    bm = min(bm, seq)
    nb = seq // bm
    gd = fw // FOURIER_GROUPS
    return pl.pallas_call(
        functools.partial(_chan_dft_body, gd=gd),
        grid=(batch, nb),
        in_specs=[pl.BlockSpec((bm, fw), lambda b, i: (b * nb + i, col_block)),
                  pl.BlockSpec((gd, gd), lambda b, i: (0, 0)),
                  pl.BlockSpec((gd, gd), lambda b, i: (0, 0))],
        out_specs=pl.BlockSpec((2, bm, fw), lambda b, i: (0, i, b)),
        out_shape=jax.ShapeDtypeStruct((2, seq, batch * fw), BF16),
        compiler_params=_params(2),
        name="channel_dft",
    )(proj, c_tbl, s_tbl)


def _fourier_out_body(f_ref, w_ref, z_ref, o_ref):
    y = jnp.dot(f_ref[...], w_ref[...], preferred_element_type=F32)
    o_ref[...] = (y * _silu(z_ref[...].astype(F32))).astype(o_ref.dtype)


def _fourier_out(f, w, proj, *, batch, seq, fw, gate_block, bm=1024):
    bm = min(bm, seq)
    nb = seq // bm
    return pl.pallas_call(
        _fourier_out_body,
        grid=(batch, nb),
        in_specs=[pl.BlockSpec((bm, fw), lambda b, i: (i, b)),
                  pl.BlockSpec((fw, fw), lambda b, i: (0, 0)),
                  pl.BlockSpec((bm, fw), lambda b, i: (b * nb + i, gate_block))],
        out_specs=pl.BlockSpec((bm, fw), lambda b, i: (b * nb + i, 0)),
        out_shape=jax.ShapeDtypeStruct((batch * seq, fw), BF16),
        compiler_params=_params(2),
        name="fourier_out",
    )(f, w, proj)


def _seq_dft_matrix(seq):
    hi = seq // DFT_SPLIT
    k = jnp.arange(seq, dtype=jnp.int32)
    a = jnp.arange(hi, dtype=jnp.int32) * DFT_SPLIT
    b = jnp.arange(DFT_SPLIT, dtype=jnp.int32)
    step = 2.0 * math.pi / seq
    ang_a = ((a[:, None] * k[None, :]) % seq).astype(F32) * step
    ang_b = ((b[:, None] * k[None, :]) % seq).astype(F32) * step
    ca, sa = jnp.cos(ang_a)[:, None, :], jnp.sin(ang_a)[:, None, :]
    cb, sb = jnp.cos(ang_b)[None, :, :], jnp.sin(ang_b)[None, :, :]
    cos = (ca * cb - sa * sb).reshape(seq, seq)
    sin = (sa * cb + ca * sb).reshape(seq, seq)
    return jnp.concatenate([cos, sin], axis=1).astype(BF16)


def _chan_dft_tables(gd, seq):
    n = jnp.arange(gd, dtype=jnp.int32)
    ang = ((n[:, None] * n[None, :]) % gd).astype(F32) * (2.0 * math.pi / gd)
    scale = 1.0 / math.sqrt(float(seq) * float(gd))
    return (jnp.cos(ang) * scale).astype(BF16), (-jnp.sin(ang) * scale).astype(BF16)


def _mem_attn_body(q_ref, z_ref, k_ref, v_ref, o_ref, *, hd):
    for h in range(MEM_HEADS):
        lanes = slice(h * hd, (h + 1) * hd)
        s = lax.dot_general(q_ref[:, lanes], k_ref[:, lanes], (((1,), (1,)), ((), ())),
                            preferred_element_type=F32)
        p = jnp.exp(s - jnp.max(s, axis=-1, keepdims=True))
        inv = 1.0 / jnp.sum(p, axis=-1, keepdims=True)
        o = jnp.dot(p.astype(BF16), v_ref[:, lanes], preferred_element_type=F32) * inv
        o_ref[:, lanes] = (o * _silu(z_ref[:, lanes].astype(F32))).astype(o_ref.dtype)


def _mem_attention(proj, km, vm, *, batch, seq, mem_len, mw, q_block, gate_block, bm=1024):
    bm = min(bm, seq)
    nb = seq // bm
    return pl.pallas_call(
        functools.partial(_mem_attn_body, hd=mw // MEM_HEADS),
        grid=(batch, nb),
        in_specs=[pl.BlockSpec((bm, mw), lambda b, i: (b * nb + i, q_block)),
                  pl.BlockSpec((bm, mw), lambda b, i: (b * nb + i, gate_block)),
                  pl.BlockSpec((mem_len, mw), lambda b, i: (b, 0)),
                  pl.BlockSpec((mem_len, mw), lambda b, i: (b, 0))],
        out_specs=pl.BlockSpec((bm, mw), lambda b, i: (b * nb + i, 0)),
        out_shape=jax.ShapeDtypeStruct((batch * seq, mw), BF16),
        compiler_params=_params(2),
        name="mem_attention",
    )(proj, proj, km, vm)


def _out_proj_body(yd_ref, yf_ref, ym_ref, w_ref, x_ref, o_ref, *, dw, fw):
    acc = jnp.dot(yd_ref[...], w_ref[0:dw, :], preferred_element_type=F32)
    acc += jnp.dot(yf_ref[...], w_ref[dw:dw + fw, :], preferred_element_type=F32)
    acc += jnp.dot(ym_ref[...], w_ref[dw + fw:, :], preferred_element_type=F32)
    o_ref[...] = x_ref[...] + acc


def _out_proj(yd, yf, ym, w, x, bm=1024, bn=512):
    m, d = x.shape
    dw, fw, mw = yd.shape[1], yf.shape[1], ym.shape[1]
    bm, bn = min(bm, m), min(bn, d)
    return pl.pallas_call(
        functools.partial(_out_proj_body, dw=dw, fw=fw),
        grid=(m // bm, d // bn),
        in_specs=[pl.BlockSpec((bm, dw), lambda i, j: (i, 0)),
                  pl.BlockSpec((bm, fw), lambda i, j: (i, 0)),
                  pl.BlockSpec((bm, mw), lambda i, j: (i, 0)),
                  pl.BlockSpec((dw + fw + mw, bn), lambda i, j: (0, j)),
                  pl.BlockSpec((bm, bn), lambda i, j: (i, j))],
        out_specs=pl.BlockSpec((bm, bn), lambda i, j: (i, j)),
        out_shape=jax.ShapeDtypeStruct((m, d), F32),
        compiler_params=_params(2),
        name="out_proj",
    )(yd, yf, ym, w, x)


def kernel(x, mem, g_in, w_in, g_mem, w_mem_k, w_mem_v, lam_q1, lam_k1, lam_q2, lam_k2, g_subln, w_fourier,
           w_out, g_final):
    batch, seq, d = x.shape
    mem_len = mem.shape[1]
    depth = w_in.shape[0]
    dw, fw = d // 2, d // 4
    mw = d - dw - fw
    heads = dw // DIFF_V_DIM
    assert fw == mw and dw == 2 * fw and dw % DIFF_V_DIM == 0 and seq % DFT_SPLIT == 0
    in_cols = 4 * dw + 2 * fw + 2 * mw
    assert w_in.shape[2] == in_cols

    col_scale = jnp.ones((in_cols,), F32)
    col_scale = col_scale.at[:dw].set(DIFF_HEAD_DIM ** -0.5)
    col_scale = col_scale.at[4 * dw + 2 * fw:4 * dw + 2 * fw + mw].set((mw // MEM_HEADS) ** -0.5)

    slopes = [2.0 ** (-8.0 * (h + 1) / heads) for h in range(heads)]
    seq_dft = _seq_dft_matrix(seq)
    c_tbl, s_tbl = _chan_dft_tables(fw // FOURIER_GROUPS, seq)
    mem2 = mem.reshape(batch * mem_len, d)

    h = x.reshape(batch * seq, d)
    for l in range(depth):
        lam_init = 0.8 - 0.6 * math.exp(-0.3 * l)
        w_in_l = (w_in[l] * col_scale[None, :]).astype(BF16)

        hn = _rmsnorm(h, g_in[l], BF16)
        proj = _matmul(hn, w_in_l, BF16, name="in_proj")

        y_d = _diff_attention(proj, slopes, lam_q1[l], lam_k1[l], lam_q2[l], lam_k2[l], g_subln[l],
                              batch=batch, seq=seq, heads=heads, lam_init=lam_init)

        g2 = _chan_dft(proj, c_tbl, s_tbl, batch=batch, seq=seq, fw=fw, col_block=(4 * dw) // fw)
        f = _matmul_acc(seq_dft, g2.reshape(2 * seq, batch * fw), BF16, name="seq_dft")
        y_f = _fourier_out(f, w_fourier[l].astype(BF16), proj, batch=batch, seq=seq, fw=fw,
                           gate_block=(4 * dw + fw) // fw)

        mn = _rmsnorm(mem2, g_mem[l], BF16)
        km = _matmul(mn, w_mem_k[l].astype(BF16), BF16, bm=512, bn=512, name="mem_k")
        vm = _matmul(mn, w_mem_v[l].astype(BF16), BF16, bm=512, bn=512, name="mem_v")
        y_m = _mem_attention(proj, km, vm, batch=batch, seq=seq, mem_len=mem_len, mw=mw,
                             q_block=(4 * dw + 2 * fw) // mw, gate_block=(4 * dw + 2 * fw + mw) // mw)

        h = _out_proj(y_d, y_f, y_m, w_out[l].astype(BF16), h)

    return _rmsnorm(h, g_final, x.dtype).reshape(batch, seq, d)
```

```python
import functools
import math

import jax
import jax.numpy as jnp
from jax import lax
from jax.experimental import pallas as pl
from jax.experimental.pallas import tpu as pltpu

RMS_EPS = 1e-6
DIFF_HEAD_DIM = 128
DIFF_V_DIM = 2 * DIFF_HEAD_DIM
FOURIER_GROUPS = 4
MEM_HEADS = 4
DFT_SPLIT = 128
SEQ_DFT_ROWS = 4
ATTN_BLOCK = 1024
ATTN_STRIP = 256
SUB_ROWS = 32
LANES = 128
FEATURE_SPLIT = 32
FAST_LIMIT = 60.0
BOUND_SLACK = 1.0 + 2.0 ** -7

V7X_VMEM_BYTES = 64 * 1024 * 1024
VMEM_LIMIT_BYTES = V7X_VMEM_BYTES - 8 * 1024 * 1024

F32 = jnp.float32
BF16 = jnp.bfloat16


def _params(n_grid_dims):
    return pltpu.CompilerParams(dimension_semantics=("arbitrary",) * n_grid_dims,
                                vmem_limit_bytes=VMEM_LIMIT_BYTES)


def _silu(z):
    return z / (1.0 + jnp.exp(-z))


def _rmsnorm_body(x_ref, g_ref, o_ref):
    x = x_ref[...].astype(F32)
    y = x * lax.rsqrt(jnp.mean(x * x, axis=-1, keepdims=True) + RMS_EPS)
    o_ref[...] = (y * g_ref[...]).astype(o_ref.dtype)


def _rmsnorm(x, g, out_dtype, bm=256):
    m, d = x.shape
    bm = min(bm, m)
    return pl.pallas_call(
        _rmsnorm_body,
        grid=(m // bm,),
        in_specs=[pl.BlockSpec((bm, d), lambda i: (i, 0)),
                  pl.BlockSpec((1, d), lambda i: (0, 0))],
        out_specs=pl.BlockSpec((bm, d), lambda i: (i, 0)),
        out_shape=jax.ShapeDtypeStruct((m, d), out_dtype),
        compiler_params=_params(1),
        name="rmsnorm",
    )(x, g.reshape(1, d).astype(F32))


def _matmul_body(a_ref, w_ref, o_ref):
    o_ref[...] = jnp.dot(a_ref[...], w_ref[...], preferred_element_type=F32).astype(o_ref.dtype)


def _matmul(a, w, out_dtype, bm=1024, bn=1024, name="matmul"):
    m, k = a.shape
    n = w.shape[1]
    bm, bn = min(bm, m), min(bn, n)
    return pl.pallas_call(
        _matmul_body,
        grid=(m // bm, n // bn),
        in_specs=[pl.BlockSpec((bm, k), lambda i, j: (i, 0)),
                  pl.BlockSpec((k, bn), lambda i, j: (0, j))],
        out_specs=pl.BlockSpec((bm, bn), lambda i, j: (i, j)),
        out_shape=jax.ShapeDtypeStruct((m, n), out_dtype),
        compiler_params=_params(2),
        name=name,
    )(a, w)


def _row_norm_body(qk_ref, o_ref, *, groups):
    lane = lax.broadcasted_iota(jnp.int32, (1, LANES), 1)
    out = jnp.zeros((1, LANES), F32)
    for g in range(groups):
        xg = qk_ref[:, g * DIFF_HEAD_DIM:(g + 1) * DIFF_HEAD_DIM].astype(F32)
        n2 = jnp.max(jnp.sum(xg * xg, axis=-1, keepdims=True), axis=0, keepdims=True)
        out = jnp.where(lane == g, n2, out)
    o_ref[0] = out


def _softmax_offsets(proj, *, batch, seq, heads, blk):
    nb = seq // blk
    g = 2 * heads
    assert 2 * g <= LANES
    part = pl.pallas_call(
        functools.partial(_row_norm_body, groups=2 * g),
        grid=(batch * nb,),
        in_specs=[pl.BlockSpec((blk, 2 * g * DIFF_HEAD_DIM), lambda i: (i, 0))],
        out_specs=pl.BlockSpec((1, 1, LANES), lambda i: (i, 0, 0)),
        out_shape=jax.ShapeDtypeStruct((batch * nb, 1, LANES), F32),
        compiler_params=_params(1),
        name="row_norms",
    )(proj)
    norms = jnp.sqrt(part.reshape(batch, nb, LANES))
    qmax = norms[:, :, :g]
    kmax = jnp.max(norms[:, :, g:2 * g], axis=1)
    bound = qmax * kmax[:, None, :] * (BOUND_SLACK * BOUND_SLACK)
    safe = jnp.all((2.0 * bound <= FAST_LIMIT).reshape(batch, nb, heads, 2), axis=-1)
    flags = jnp.transpose(safe, (0, 2, 1)).astype(jnp.int32).reshape(batch * heads * nb)
    return kmax.reshape(batch * g), flags


def _diff_attn_body(slopes_ref, kmax_ref, fast_ref, q_ref, k_ref, v_ref, z_ref, lq1_ref, lk1_ref, lq2_ref,
                    lk2_ref, g_ref, o_ref, tbl_ref, t0_ref, t1_ref, p0_ref, p1_ref, pm_ref, msub_ref,
                    qaug_ref, kaug_ref, kbase_ref, m_ref, lp_ref, acc_ref,
                    *, blk, strip, nk, heads, lam_init, exact_slopes):
    b = pl.program_id(0)
    h = pl.program_id(1)
    qi = pl.program_id(2)
    ki = pl.program_id(3)
    slope = slopes_ref[h]
    n_chunks = blk // LANES
    lane = lax.broadcasted_iota(jnp.int32, (blk, LANES), 1)
    row = lax.broadcasted_iota(jnp.int32, (blk, LANES), 0)
    row_lo = (row & (FEATURE_SPLIT - 1)).astype(F32)
    row_hi = (row & ~(FEATURE_SPLIT - 1)).astype(F32)

    @pl.when((qi == 0) & (ki == 0))
    def _fill_tables():
        r2 = lax.broadcasted_iota(jnp.int32, (blk, blk), 0)
        c2 = lax.broadcasted_iota(jnp.int32, (blk, blk), 1)
        d = (r2 - c2).astype(F32) * slope
        tbl_ref[0] = d
        tbl_ref[1] = -d
        tbl_ref[2] = jnp.abs(d)
        for sel, sign in ((0, -1.0), (1, 1.0), (2, 0.0)):
            s = sign * slope
            kb = jnp.where(lane <= 1, s,
                           jnp.where(lane == 2, -s * row_lo,
                                     jnp.where(lane == 3, -s * row_hi,
                                               jnp.where(lane == 4, 1.0, 0.0))))
            kbase_ref[sel] = kb

    @pl.when(ki == 0)
    def _reset_state():
        m_ref[...] = jnp.full_like(m_ref, -jnp.inf)
        lp_ref[...] = jnp.zeros_like(lp_ref)
        acc_ref[...] = jnp.zeros_like(acc_ref)
        for cm in range(2):
            lanes = slice(cm * DIFF_HEAD_DIM, (cm + 1) * DIFF_HEAD_DIM)
            qf = q_ref[:, lanes].astype(F32)
            qn = jnp.sqrt(jnp.sum(qf * qf, axis=-1, keepdims=True))
            bound = qn * (kmax_ref[(b * heads + h) * 2 + cm] * BOUND_SLACK)
            m_fix = jnp.broadcast_to(bound, (blk, LANES)).astype(BF16).astype(F32)
            qfeat = jnp.where(lane == 0, row_lo,
                              jnp.where(lane == 1, row_hi,
                                        jnp.where(lane == 4, -m_fix,
                                                  jnp.where((lane == 2) | (lane == 3) | (lane == 5), 1.0, 0.0))))
            qaug_ref[cm, :, 0:DIFF_HEAD_DIM] = q_ref[:, lanes]
            qaug_ref[cm, :, DIFF_HEAD_DIM:] = qfeat.astype(BF16)

    sel = jnp.where(ki < qi, 0, jnp.where(ki > qi, 1, 2))
    c = slope * (jnp.abs(qi - ki) * blk).astype(F32)
    fast = (fast_ref[(b * heads + h) * nk + qi] == 1) & exact_slopes

    def _fast_step(diag):
        kfeat = jnp.where(lane == 5, -c, kbase_ref[sel]).astype(BF16)
        for cm in range(2):
            lanes = slice(cm * DIFF_HEAD_DIM, (cm + 1) * DIFF_HEAD_DIM)
            kaug_ref[cm, :, 0:DIFF_HEAD_DIM] = k_ref[:, lanes]
            kaug_ref[cm, :, DIFF_HEAD_DIM:] = kfeat
        for r in range(blk // strip):
            rows = slice(r * strip, (r + 1) * strip)
            for cm in range(2):
                x = lax.dot_general(qaug_ref[cm, rows, :], kaug_ref[cm], (((1,), (1,)), ((), ())),
                                    preferred_element_type=F32)
                if diag:
                    x = x - tbl_ref[2, rows, :]
                p = jnp.exp(x)
                ps = p[:, 0:LANES]
                for j in range(1, n_chunks):
                    ps = ps + p[:, j * LANES:(j + 1) * LANES]
                lp_ref[cm, rows, :] += ps
                acc_ref[cm, rows, :] += jnp.dot(p.astype(BF16), v_ref[...], preferred_element_type=F32)

    @pl.when(fast & (ki != qi))
    def _fast_off_diagonal():
        _fast_step(False)

    @pl.when(fast & (ki == qi))
    def _fast_diagonal():
        _fast_step(True)

    @pl.when(jnp.logical_not(fast))
    def _general_step():
        slot = jnp.minimum(ki, 0)
        for r in range(blk // strip):
            rows = slice(r * strip, (r + 1) * strip)
            for cm, t_ref, p_ref in ((0, t0_ref, p0_ref), (1, t1_ref, p1_ref)):
                lanes = slice(cm * DIFF_HEAD_DIM, (cm + 1) * DIFF_HEAD_DIM)
                t_ref[slot, rows, :] = lax.dot_general(
                    q_ref[rows, lanes], k_ref[:, lanes], (((1,), (1,)), ((), ())),
                    preferred_element_type=F32) - tbl_ref[sel, rows, :]

                for s in range(strip // SUB_ROWS):
                    srows = slice(r * strip + s * SUB_ROWS, r * strip + (s + 1) * SUB_ROWS)
                    pm = t_ref[slot, srows, 0:LANES]
                    for j in range(1, n_chunks):
                        pm = jnp.maximum(pm, t_ref[slot, srows, j * LANES:(j + 1) * LANES])
                    pm_ref[cm, srows, :] = pm

                m_old = m_ref[cm, rows, :]
                m_new = jnp.maximum(m_old, jnp.max(pm_ref[cm, rows, :], axis=-1, keepdims=True) - c)
                alpha = jnp.exp(m_old - m_new)
                m_ref[cm, rows, :] = m_new
                msub_ref[cm, rows, :] = jnp.broadcast_to(m_new + c, (strip, LANES))

                for s in range(strip // SUB_ROWS):
                    srows = slice(r * strip + s * SUB_ROWS, r * strip + (s + 1) * SUB_ROWS)
                    msub = msub_ref[cm, srows, :]
                    ps = None
                    for j in range(n_chunks):
                        chunk = slice(j * LANES, (j + 1) * LANES)
                        p = jnp.exp(t_ref[slot, srows, chunk] - msub)
                        ps = p if ps is None else ps + p
                        p_ref[slot, srows, chunk] = p.astype(BF16)
                    pm_ref[cm, srows, :] = ps

                lp_ref[cm, rows, :] = alpha * lp_ref[cm, rows, :] + pm_ref[cm, rows, :]
                acc_ref[cm, rows, :] = alpha * acc_ref[cm, rows, :] + jnp.dot(
                    p_ref[slot, rows, :], v_ref[...], preferred_element_type=F32)

    @pl.when(ki == nk - 1)
    def _finish():
        lam = (jnp.exp(jnp.sum(lq1_ref[...] * lk1_ref[...], axis=-1, keepdims=True))
               - jnp.exp(jnp.sum(lq2_ref[...] * lk2_ref[...], axis=-1, keepdims=True)) + lam_init)
        l0 = jnp.sum(lp_ref[0], axis=-1, keepdims=True)
        l1 = jnp.sum(lp_ref[1], axis=-1, keepdims=True)
        o = acc_ref[0] * (1.0 / l0) - lam * (acc_ref[1] * (1.0 / l1))
        o = o * lax.rsqrt(jnp.mean(o * o, axis=-1, keepdims=True) + RMS_EPS)
        o = o * g_ref[...] * (1.0 - lam_init)
        o_ref[...] = (o * _silu(z_ref[...].astype(F32))).astype(o_ref.dtype)


def _diff_attention(proj, slope_values, lam_q1, lam_k1, lam_q2, lam_k2, g_subln, *, batch, seq, heads,
                    lam_init, blk=None):
    blk = min(blk or ATTN_BLOCK, seq)
    nb = seq // blk
    hv = DIFF_V_DIM
    exact_slopes = all(math.frexp(s)[0] == 0.5 for s in slope_values) and nb <= 256 and blk <= 8192
    slopes = jnp.asarray(slope_values, dtype=F32)
    kmax, fast_flags = _softmax_offsets(proj, batch=batch, seq=seq, heads=heads, blk=blk)

    def rows_q(b, h, qi, ki, *_):
        return b * nb + qi

    def rows_k(b, h, qi, ki, *_):
        return b * nb + ki

    vec = lambda n: pl.BlockSpec((1, n), lambda b, h, qi, ki, *_: (0, 0))
    grid_spec = pltpu.PrefetchScalarGridSpec(
        num_scalar_prefetch=3,
        grid=(batch, heads, nb, nb),
        in_specs=[
            pl.BlockSpec((blk, hv), lambda b, h, qi, ki, *_: (rows_q(b, h, qi, ki), h)),
            pl.BlockSpec((blk, hv), lambda b, h, qi, ki, *_: (rows_k(b, h, qi, ki), heads + h)),
            pl.BlockSpec((blk, hv), lambda b, h, qi, ki, *_: (rows_k(b, h, qi, ki), 2 * heads + h)),
            pl.BlockSpec((blk, hv), lambda b, h, qi, ki, *_: (rows_q(b, h, qi, ki), 3 * heads + h)),
            vec(DIFF_HEAD_DIM), vec(DIFF_HEAD_DIM), vec(DIFF_HEAD_DIM), vec(DIFF_HEAD_DIM), vec(hv),
        ],
        out_specs=pl.BlockSpec((blk, hv), lambda b, h, qi, ki, *_: (rows_q(b, h, qi, ki), h)),
        scratch_shapes=[pltpu.VMEM((3, blk, blk), F32),
                        pltpu.VMEM((1, blk, blk), F32),
                        pltpu.VMEM((1, blk, blk), F32),
                        pltpu.VMEM((1, blk, blk), BF16),
                        pltpu.VMEM((1, blk, blk), BF16),
                        pltpu.VMEM((2, blk, LANES), F32),
                        pltpu.VMEM((2, blk, LANES), F32),
                        pltpu.VMEM((2, blk, 2 * DIFF_HEAD_DIM), BF16),
                        pltpu.VMEM((2, blk, 2 * DIFF_HEAD_DIM), BF16),
                        pltpu.VMEM((3, blk, LANES), F32),
                        pltpu.VMEM((2, blk, 1), F32),
                        pltpu.VMEM((2, blk, LANES), F32),
                        pltpu.VMEM((2, blk, hv), F32)],
    )
    row = lambda a: a.reshape(1, -1).astype(F32)
    return pl.pallas_call(
        functools.partial(_diff_attn_body, blk=blk, strip=min(ATTN_STRIP, blk), nk=nb, heads=heads,
                          lam_init=lam_init, exact_slopes=exact_slopes),
        grid_spec=grid_spec,
        out_shape=jax.ShapeDtypeStruct((batch * seq, heads * hv), BF16),
        compiler_params=_params(4),
        name="diff_attention",
    )(slopes, kmax, fast_flags, proj, proj, proj, proj, row(lam_q1), row(lam_k1), row(lam_q2), row(lam_k2),
      row(g_subln))


def _chan_dft_body(u_ref, c_ref, s_ref, o_ref, *, gd):
    for g in range(FOURIER_GROUPS):
        lanes = slice(g * gd, (g + 1) * gd)
        ug = u_ref[:, lanes]
        o_ref[0, :, lanes] = jnp.dot(ug, c_ref[...], preferred_element_type=F32).astype(o_ref.dtype)
        o_ref[1, :, lanes] = jnp.dot(ug, s_ref[...], preferred_element_type=F32).astype(o_ref.dtype)


def _chan_dft(proj, c_tbl, s_tbl, *, batch, seq, fw, col_block, bm=1024):
    bm = min(bm, seq)
    nb = seq // bm
    gd = fw // FOURIER_GROUPS
    return pl.pallas_call(
        functools.partial(_chan_dft_body, gd=gd),
        grid=(batch, nb),
        in_specs=[pl.BlockSpec((bm, fw), lambda b, i: (b * nb + i, col_block)),
                  pl.BlockSpec((gd, gd), lambda b, i: (0, 0)),
                  pl.BlockSpec((gd, gd), lambda b, i: (0, 0))],
        out_specs=pl.BlockSpec((2, bm, fw), lambda b, i: (0, i, b)),
        out_shape=jax.ShapeDtypeStruct((2, seq, batch * fw), BF16),
        compiler_params=_params(2),
        name="channel_dft",
    )(proj, c_tbl, s_tbl)


def _fourier_out_body(f_ref, w_ref, z_ref, o_ref):
    y = jnp.dot(f_ref[...], w_ref[...], preferred_element_type=F32)
    o_ref[...] = (y * _silu(z_ref[...].astype(F32))).astype(o_ref.dtype)


def _fourier_out(f, w, proj, *, batch, seq, fw, gate_block, bm=1024):
    bm = min(bm, seq)
    nb = seq // bm
    return pl.pallas_call(
        _fourier_out_body,
        grid=(batch, nb),
        in_specs=[pl.BlockSpec((bm, fw), lambda b, i: (i, b)),
                  pl.BlockSpec((fw, fw), lambda b, i: (0, 0)),
                  pl.BlockSpec((bm, fw), lambda b, i: (b * nb + i, gate_block))],
        out_specs=pl.BlockSpec((bm, fw), lambda b, i: (b * nb + i, 0)),
        out_shape=jax.ShapeDtypeStruct((batch * seq, fw), BF16),
        compiler_params=_params(2),
        name="fourier_out",
    )(f, w, proj)


def _seq_dft_tables(seq):
    n1_count = seq // DFT_SPLIT
    n1 = jnp.arange(n1_count, dtype=jnp.int32)
    ang = ((n1[:, None] * n1[None, :]) % n1_count).astype(F32) * (2.0 * math.pi / n1_count)
    c, s = jnp.cos(ang), jnp.sin(ang)
    stage1 = jnp.concatenate([jnp.concatenate([c, s], axis=1),
                              jnp.concatenate([-s, c], axis=1)], axis=0).astype(BF16)
    k2 = jnp.arange(DFT_SPLIT, dtype=jnp.int32)
    n2 = jnp.arange(DFT_SPLIT, dtype=jnp.int32)
    k = n1[:, None, None] + n1_count * k2[None, :, None]
    phi = ((k * n2[None, None, :]) % seq).astype(F32) * (2.0 * math.pi / seq)
    stage2 = jnp.concatenate([jnp.cos(phi), jnp.sin(phi)], axis=-1).astype(BF16)
    return stage1, stage2


def _seq_dft_stage1_body(t_ref, z_ref, o_ref, *, n1_count):
    z = jnp.concatenate([z_ref[0], z_ref[1]], axis=0)
    y = jnp.dot(t_ref[...], z, preferred_element_type=F32)
    o_ref[0] = y[:n1_count].astype(o_ref.dtype)
    o_ref[1] = y[n1_count:].astype(o_ref.dtype)


def _seq_dft_stage2_body(t_ref, y_ref, o_ref, *, rows):
    for r in range(rows):
        y = jnp.concatenate([y_ref[0, r], y_ref[1, r]], axis=0)
        o_ref[r] = jnp.dot(t_ref[r], y, preferred_element_type=F32).astype(o_ref.dtype)


def _seq_dft(g2, stage1, stage2, *, batch, seq, fw, cols_per_step=8192):
    n1_count = seq // DFT_SPLIT
    cols = DFT_SPLIT * batch * fw
    cb = min(cols_per_step, cols)
    y = pl.pallas_call(
        functools.partial(_seq_dft_stage1_body, n1_count=n1_count),
        grid=(cols // cb,),
        in_specs=[pl.BlockSpec((2 * n1_count, 2 * n1_count), lambda j: (0, 0)),
                  pl.BlockSpec((2, n1_count, cb), lambda j: (0, 0, j))],
        out_specs=pl.BlockSpec((2, n1_count, cb), lambda j: (0, 0, j)),
        out_shape=jax.ShapeDtypeStruct((2, n1_count, cols), BF16),
        compiler_params=_params(1),
        name="seq_dft_stage1",
    )(stage1, g2.reshape(2, n1_count, cols))
    rows = min(SEQ_DFT_ROWS, n1_count)
    xt = pl.pallas_call(
        functools.partial(_seq_dft_stage2_body, rows=rows),
        grid=(batch, n1_count // rows),
        in_specs=[pl.BlockSpec((rows, DFT_SPLIT, 2 * DFT_SPLIT), lambda b, j: (j, 0, 0)),
                  pl.BlockSpec((2, rows, DFT_SPLIT, fw), lambda b, j: (0, j, 0, b))],
        out_specs=pl.BlockSpec((rows, DFT_SPLIT, fw), lambda b, j: (j, 0, b)),
        out_shape=jax.ShapeDtypeStruct((n1_count, DFT_SPLIT, batch * fw), BF16),
        compiler_params=_params(2),
        name="seq_dft_stage2",
    )(stage2, y.reshape(2, n1_count, DFT_SPLIT, batch * fw))
    return jnp.transpose(xt, (1, 0, 2)).reshape(seq, batch * fw)


def _chan_dft_tables(gd, seq):
    n = jnp.arange(gd, dtype=jnp.int32)
    ang = ((n[:, None] * n[None, :]) % gd).astype(F32) * (2.0 * math.pi / gd)
    scale = 1.0 / math.sqrt(float(seq) * float(gd))
    return (jnp.cos(ang) * scale).astype(BF16), (-jnp.sin(ang) * scale).astype(BF16)


def _mem_attn_body(q_ref, z_ref, k_ref, v_ref, o_ref, *, hd):
    for h in range(MEM_HEADS):
        lanes = slice(h * hd, (h + 1) * hd)
        s = lax.dot_general(q_ref[:, lanes], k_ref[:, lanes], (((1,), (1,)), ((), ())),
                            preferred_element_type=F32)
        p = jnp.exp(s - jnp.max(s, axis=-1, keepdims=True))
        inv = 1.0 / jnp.sum(p, axis=-1, keepdims=True)
        o = jnp.dot(p.astype(BF16), v_ref[:, lanes], preferred_element_type=F32) * inv
        o_ref[:, lanes] = (o * _silu(z_ref[:, lanes].astype(F32))).astype(o_ref.dtype)


def _mem_attention(proj, km, vm, *, batch, seq, mem_len, mw, q_block, gate_block, bm=1024):
    bm = min(bm, seq)
    nb = seq // bm
    return pl.pallas_call(
        functools.partial(_mem_attn_body, hd=mw // MEM_HEADS),
        grid=(batch, nb),
        in_specs=[pl.BlockSpec((bm, mw), lambda b, i: (b * nb + i, q_block)),
                  pl.BlockSpec((bm, mw), lambda b, i: (b * nb + i, gate_block)),
                  pl.BlockSpec((mem_len, mw), lambda b, i: (b, 0)),
                  pl.BlockSpec((mem_len, mw), lambda b, i: (b, 0))],
        out_specs=pl.BlockSpec((bm, mw), lambda b, i: (b * nb + i, 0)),
        out_shape=jax.ShapeDtypeStruct((batch * seq, mw), BF16),
        compiler_params=_params(2),
        name="mem_attention",
    )(proj, proj, km, vm)


def _out_proj_body(yd_ref, yf_ref, ym_ref, w_ref, x_ref, o_ref, *, dw, fw):
    acc = jnp.dot(yd_ref[...], w_ref[0:dw, :], preferred_element_type=F32)
    acc += jnp.dot(yf_ref[...], w_ref[dw:dw + fw, :], preferred_element_type=F32)
    acc += jnp.dot(ym_ref[...], w_ref[dw + fw:, :], preferred_element_type=F32)
    o_ref[...] = x_ref[...] + acc


def _out_proj(yd, yf, ym, w, x, bm=1024, bn=512):
    m, d = x.shape
    dw, fw, mw = yd.shape[1], yf.shape[1], ym.shape[1]
    bm, bn = min(bm, m), min(bn, d)
    return pl.pallas_call(
        functools.partial(_out_proj_body, dw=dw, fw=fw),
        grid=(m // bm, d // bn),
        in_specs=[pl.BlockSpec((bm, dw), lambda i, j: (i, 0)),
                  pl.BlockSpec((bm, fw), lambda i, j: (i, 0)),
                  pl.BlockSpec((bm, mw), lambda i, j: (i, 0)),
                  pl.BlockSpec((dw + fw + mw, bn), lambda i, j: (0, j)),
                  pl.BlockSpec((bm, bn), lambda i, j: (i, j))],
        out_specs=pl.BlockSpec((bm, bn), lambda i, j: (i, j)),
        out_shape=jax.ShapeDtypeStruct((m, d), F32),
        compiler_params=_params(2),
        name="out_proj",
    )(yd, yf, ym, w, x)


def kernel(x, mem, g_in, w_in, g_mem, w_mem_k, w_mem_v, lam_q1, lam_k1, lam_q2, lam_k2, g_subln, w_fourier,
           w_out, g_final):
    batch, seq, d = x.shape
    mem_len = mem.shape[1]
    depth = w_in.shape[0]
    dw, fw = d // 2, d // 4
    mw = d - dw - fw
    heads = dw // DIFF_V_DIM
    assert fw == mw and dw == 2 * fw and dw % DIFF_V_DIM == 0 and seq % DFT_SPLIT == 0
    in_cols = 4 * dw + 2 * fw + 2 * mw
    assert w_in.shape[2] == in_cols

    col_scale = jnp.ones((in_cols,), F32)
    col_scale = col_scale.at[:dw].set(DIFF_HEAD_DIM ** -0.5)
    col_scale = col_scale.at[4 * dw + 2 * fw:4 * dw + 2 * fw + mw].set((mw // MEM_HEADS) ** -0.5)

    slopes = [2.0 ** (-8.0 * (h + 1) / heads) for h in range(heads)]
    dft_stage1, dft_stage2 = _seq_dft_tables(seq)
    c_tbl, s_tbl = _chan_dft_tables(fw // FOURIER_GROUPS, seq)
    mem2 = mem.reshape(batch * mem_len, d)

    h = x.reshape(batch * seq, d)
    for l in range(depth):
        lam_init = 0.8 - 0.6 * math.exp(-0.3 * l)
        w_in_l = (w_in[l] * col_scale[None, :]).astype(BF16)

        hn = _rmsnorm(h, g_in[l], BF16)
        proj = _matmul(hn, w_in_l, BF16, name="in_proj")

        y_d = _diff_attention(proj, slopes, lam_q1[l], lam_k1[l], lam_q2[l], lam_k2[l], g_subln[l],
                              batch=batch, seq=seq, heads=heads, lam_init=lam_init)

        g2 = _chan_dft(proj, c_tbl, s_tbl, batch=batch, seq=seq, fw=fw, col_block=(4 * dw) // fw)
        f = _seq_dft(g2, dft_stage1, dft_stage2, batch=batch, seq=seq, fw=fw)
        y_f = _fourier_out(f, w_fourier[l].astype(BF16), proj, batch=batch, seq=seq, fw=fw,
                           gate_block=(4 * dw + fw) // fw)

        mn = _rmsnorm(mem2, g_mem[l], BF16)
        km = _matmul(mn, w_mem_k[l].astype(BF16), BF16, bm=512, bn=512, name="mem_k")
        vm = _matmul(mn, w_mem_v[l].astype(BF16), BF16, bm=512, bn=512, name="mem_v")
        y_m = _mem_attention(proj, km, vm, batch=batch, seq=seq, mem_len=mem_len, mw=mw,
                             q_block=(4 * dw + 2 * fw) // mw, gate_block=(4 * dw + 2 * fw + mw) // mw)

        h = _out_proj(y_d, y_f, y_m, w_out[l].astype(BF16), h)

    return _rmsnorm(h, g_final, x.dtype).reshape(batch, seq, d)
```
